```python
import math
import jax, jax.numpy as jnp
from jax import lax
import numpy as np

D_MODEL = 4096
BATCH = 32
SEQ = 256
DEPTH = 2
DEC_BATCH = 4
DEC_SEQ = 4096
PAST_LEN = 256

GRID_W = 64
MIX_WIDTH = D_MODEL
ATTN_WIDTH = MIX_WIDTH // 2
SSM_WIDTH = MIX_WIDTH - ATTN_WIDTH
ATTN_HEADS = 16
ATTN_DK = ATTN_WIDTH // (2 * ATTN_HEADS)
ATTN_DV = 2 * ATTN_DK
SSM_HEADDIM = 64
SSM_HEADS = SSM_WIDTH // SSM_HEADDIM
SSM_GROUPS = 8
SSM_STATE = 128
CONV_W = 5
CHUNK = 128
Q_BLOCK = 128
D_FF = 11008
ROPE_THETA = 10000.0
LN_EPS = 1e-5
RMS_EPS = 1e-6
ALPHA = (2 * DEPTH) ** 0.25
BETA = (8 * DEPTH) ** -0.25
N_SUB = 3
Q_COLS = ATTN_HEADS * 2 * ATTN_DK
K_COLS = ATTN_HEADS * 2 * ATTN_DK
V_COLS = ATTN_HEADS * ATTN_DV
Z_COLS = SSM_WIDTH
BC_COLS = SSM_GROUPS * SSM_STATE
CONV_CH = SSM_WIDTH + 2 * BC_COLS
DT_COLS = 2 * SSM_HEADS
IN_COLS = Q_COLS + K_COLS + V_COLS + Z_COLS + CONV_CH + DT_COLS

kernel_name = "hybrid_diffattn_ssd_prefix_dit"


def _layer_norm(x, g, b):
    xf = x.astype(jnp.float32)
    mu = jnp.mean(xf, axis=-1, keepdims=True)
    xc = xf - mu
    var = jnp.mean(xc * xc, axis=-1, keepdims=True)
    return (xc * lax.rsqrt(var + LN_EPS) * g.astype(jnp.float32) + b.astype(jnp.float32)).astype(x.dtype)


def _rms_norm(x, g):
    xf = x.astype(jnp.float32)
    y = xf * lax.rsqrt(jnp.mean(xf * xf, axis=-1, keepdims=True) + RMS_EPS)
    return (y * g.astype(jnp.float32)).astype(x.dtype)


def _swiglu(h, w_up, w_down):
    g, u = jnp.split(h @ w_up, 2, axis=-1)
    return (jax.nn.silu(g) * u) @ w_down


def _modulation(cvec, w_mod_l, b_mod_l):
    m = jax.nn.silu(cvec) @ w_mod_l + b_mod_l
    return m.reshape(cvec.shape[0], N_SUB, 3, D_MODEL)[:, :, :, None, :]


def _depthwise_conv_centred(x, w, b):
    pad = (CONV_W - 1) // 2
    y = lax.conv_general_dilated(x, w[:, None, :].astype(x.dtype), window_strides=(1,),
                                 padding=[(pad, pad)], dimension_numbers=('NWC', 'WIO', 'NWC'),
                                 feature_group_count=x.shape[-1])
    return y + b


def _axial_rope_tables(n_tokens):
    rows = n_tokens // GRID_W
    row = jnp.repeat(jnp.arange(rows, dtype=jnp.float32), GRID_W)
    col = jnp.tile(jnp.arange(GRID_W, dtype=jnp.float32), rows)
    n_freq = ATTN_DK // 4
    inv_freq = ROPE_THETA ** (-jnp.arange(n_freq, dtype=jnp.float32) / n_freq)
    ang_r = row[:, None] * inv_freq
    ang_c = col[:, None] * inv_freq
    return (jnp.cos(ang_r), jnp.sin(ang_r), jnp.cos(ang_c), jnp.sin(ang_c))


def _rope_half(x, cos, sin):
    x1, x2 = jnp.split(x, 2, axis=-1)
    c = cos[:, None, None, :]
    s = sin[:, None, None, :]
    return jnp.concatenate([x1 * c - x2 * s, x1 * s + x2 * c], axis=-1)


def _apply_axial_rope(x, tables):
    cos_r, sin_r, cos_c, sin_c = tables
    xr, xc = jnp.split(x, 2, axis=-1)
    return jnp.concatenate([_rope_half(xr, cos_r, sin_r), _rope_half(xc, cos_c, sin_c)], axis=-1).astype(x.dtype)


def _diff_attention(q, k, v, lam):
    b, lq = q.shape[:2]
    nblk = lq // Q_BLOCK
    qb = jnp.moveaxis(q.reshape(b, nblk, Q_BLOCK, ATTN_HEADS, 2, ATTN_DK), 1, 0)
    scale = ATTN_DK ** -0.5

    def block(qblk):
        s = jnp.einsum('bqhmd,bkhmd->bhmqk', qblk, k).astype(jnp.float32) * scale
        p = jax.nn.softmax(s, axis=-1)
        a = p[:, :, 0] - lam * p[:, :, 1]
        return jnp.einsum('bhqk,bkhe->bqhe', a.astype(v.dtype), v)

    o = lax.map(block, qb)
    return jnp.moveaxis(o, 0, 1).reshape(b, lq, ATTN_HEADS, ATTN_DV)


def _ssd_chunked(x, dt, a, bm, cm, h0):
    b, L = x.shape[:2]
    nc = L // CHUNK
    hpg = SSM_HEADS // SSM_GROUPS
    f32 = jnp.float32
    xc = x.astype(f32).reshape(b, nc, CHUNK, SSM_HEADS, SSM_HEADDIM)
    bc = jnp.repeat(bm.astype(f32), hpg, axis=2).reshape(b, nc, CHUNK, SSM_HEADS, SSM_STATE)
    cc = jnp.repeat(cm.astype(f32), hpg, axis=2).reshape(b, nc, CHUNK, SSM_HEADS, SSM_STATE)
    dtc = dt.reshape(b, nc, CHUNK, SSM_HEADS)
    acum = jnp.cumsum(dtc * a, axis=2)
    seg = acum[:, :, :, None, :] - acum[:, :, None, :, :]
    lower = jnp.tril(jnp.ones((CHUNK, CHUNK), dtype=bool))[None, None, :, :, None]
    decay = jnp.exp(jnp.where(lower, seg, -jnp.inf))
    scores = jnp.einsum('bcihn,bcjhn->bcijh', cc, bc) * decay * dtc[:, :, None, :, :]
    y_intra = jnp.einsum('bcijh,bcjhp->bcihp', scores, xc)
    w_end = jnp.exp(acum[:, :, -1:, :] - acum) * dtc
    s_chunk = jnp.einsum('bcjh,bcjhn,bcjhp->bchpn', w_end, bc, xc)
    chunk_decay = jnp.exp(acum[:, :, -1, :])

    def step(h, inp):
        s_c, d_c = inp
        return d_c[:, :, None, None] * h + s_c, h

    h_final, h_start = lax.scan(step, h0.astype(f32),
                                (jnp.moveaxis(s_chunk, 1, 0), jnp.moveaxis(chunk_decay, 1, 0)))
    h_start = jnp.moveaxis(h_start, 0, 1)
    y_inter = jnp.einsum('bcihn,bchpn->bcihp', cc, h_start) * jnp.exp(acum)[..., None]
    y = (y_intra + y_inter).reshape(b, L, SSM_HEADS, SSM_HEADDIM)
    return y.astype(x.dtype), h_final


def _trunk_layer(x, mod, l, p, ctx_k=None, ctx_v=None, h0_f=None, h0_b=None):
    latent = ctx_k is not None
    b, L, _ = x.shape
    ln_g, ln_b = p['ln_g'][l], p['ln_b'][l]

    h = x * (1.0 + mod[:, 0, 1]) + mod[:, 0, 0]
    out = _swiglu(h, p['w_ffn_up'][l, 0], p['w_ffn_down'][l, 0])
    x = _layer_norm(ALPHA * x + 0.5 * mod[:, 0, 2] * out, ln_g[0], ln_b[0])

    h = x * (1.0 + mod[:, 1, 1]) + mod[:, 1, 0]
    proj = h @ p['w_in'][l]
    q, k, v, z, xbc, dt = jnp.split(proj, np.cumsum([Q_COLS, K_COLS, V_COLS, Z_COLS, CONV_CH]).tolist(), axis=-1)
    q = q.reshape(b, L, ATTN_HEADS, 2, ATTN_DK)
    k = k.reshape(b, L, ATTN_HEADS, 2, ATTN_DK)
    v = v.reshape(b, L, ATTN_HEADS, ATTN_DV)

    lambda_init = 0.8 - 0.6 * math.exp(-0.3 * l)
    f32 = jnp.float32
    lam = (jnp.exp(jnp.sum(p['lambda_q1'][l].astype(f32) * p['lambda_k1'][l].astype(f32)))
           - jnp.exp(jnp.sum(p['lambda_q2'][l].astype(f32) * p['lambda_k2'][l].astype(f32))) + lambda_init)
    if latent:
        tables = _axial_rope_tables(L)
        q = _apply_axial_rope(q, tables)
        k_all = jnp.concatenate([ctx_k.astype(k.dtype), _apply_axial_rope(k, tables)], axis=1)
        v_all = jnp.concatenate([ctx_v.astype(v.dtype), v], axis=1)
    else:
        k_all, v_all = k, v
    attn = _diff_attention(q, k_all, v_all, lam)
    attn = (_rms_norm(attn, p['subln_g'][l]) * (1.0 - lambda_init)).reshape(b, L, ATTN_WIDTH)

    xbc = jax.nn.silu(_depthwise_conv_centred(xbc, p['conv_w'][l], p['conv_b'][l]))
    xs, bm, cm = jnp.split(xbc, [SSM_WIDTH, SSM_WIDTH + BC_COLS], axis=-1)
    xs = xs.reshape(b, L, SSM_HEADS, SSM_HEADDIM)
    bm = bm.reshape(b, L, SSM_GROUPS, SSM_STATE)
    cm = cm.reshape(b, L, SSM_GROUPS, SSM_STATE)
    dt = jax.nn.softplus(dt.astype(f32) + p['dt_bias'][l].reshape(DT_COLS).astype(f32))
    dt_f, dt_b = jnp.split(dt, 2, axis=-1)
    a_f = -jnp.exp(p['a_log'][l, 0].astype(f32))
    a_b = -jnp.exp(p['a_log'][l, 1].astype(f32))
    if not latent:
        h0_f = jnp.zeros((b, SSM_HEADS, SSM_HEADDIM, SSM_STATE), f32)
        h0_b = jnp.zeros((b, SSM_HEADS, SSM_HEADDIM, SSM_STATE), f32)
    y_f, hf = _ssd_chunked(xs, dt_f, a_f, bm, cm, h0_f)
    y_b, hb = _ssd_chunked(jnp.flip(xs, 1), jnp.flip(dt_b, 1), a_b, jnp.flip(bm, 1), jnp.flip(cm, 1), h0_b)
    y = y_f + jnp.flip(y_b, 1) + p['d_skip'][l][:, None] * xs
    ssd = _rms_norm(y.reshape(b, L, SSM_WIDTH) * jax.nn.silu(z), p['ssm_norm_g'][l])

    out = jnp.concatenate([attn, ssd], axis=-1) @ p['w_out'][l]
    x = _layer_norm(ALPHA * x + mod[:, 1, 2] * out, ln_g[1], ln_b[1])

    h = x * (1.0 + mod[:, 2, 1]) + mod[:, 2, 0]
    out = _swiglu(h, p['w_ffn_up'][l, 1], p['w_ffn_down'][l, 1])
    x = _layer_norm(ALPHA * x + 0.5 * mod[:, 2, 2] * out, ln_g[2], ln_b[2])
    return x, k, v, jnp.stack([hf, hb], axis=1)


def setup_inputs(seed: int = 0) -> dict:
    key = jax.random.key(seed)
    ks = jax.random.split(key, 32)

    def nrm(k, shape, s):
        return jax.random.normal(k, shape, jnp.float32) * s

    dt0 = jnp.exp(jax.random.uniform(ks[16], (DEPTH, 2, SSM_HEADS), jnp.float32,
                                     minval=math.log(1e-3), maxval=math.log(1e-1)))
    return {
        'x_prompt': nrm(ks[0], (BATCH, SEQ, D_MODEL), 1.0),
        'x_sample': nrm(ks[1], (DEC_BATCH, DEC_SEQ, D_MODEL), 1.0),
        'c': nrm(ks[2], (DEC_BATCH, D_MODEL), 1.0),
        'cache_k': nrm(ks[3], (DEC_BATCH, DEPTH, PAST_LEN, ATTN_HEADS, 2, ATTN_DK), 1.0),
        'cache_v': nrm(ks[4], (DEC_BATCH, DEPTH, PAST_LEN, ATTN_HEADS, ATTN_DV), 1.0),
        'state_ssm': nrm(ks[5], (DEC_BATCH, DEPTH, 2, SSM_HEADS, SSM_HEADDIM, SSM_STATE), 0.1),
        'c_ctx': nrm(ks[6], (D_MODEL,), 1.0),
        'w_mod': nrm(ks[7], (DEPTH, D_MODEL, N_SUB * 3 * D_MODEL), D_MODEL ** -0.5),
        'b_mod': nrm(ks[8], (DEPTH, N_SUB * 3 * D_MODEL), 0.02),
        'ln_g': 1.0 + nrm(ks[9], (DEPTH, N_SUB, D_MODEL), 0.02),
        'ln_b': nrm(ks[10], (DEPTH, N_SUB, D_MODEL), 0.02),
        'w_ffn_up': nrm(ks[11], (DEPTH, 2, D_MODEL, 2 * D_FF), D_MODEL ** -0.5),
        'w_ffn_down': nrm(ks[12], (DEPTH, 2, D_FF, D_MODEL), BETA * D_FF ** -0.5),
        'w_in': nrm(ks[13], (DEPTH, D_MODEL, IN_COLS), D_MODEL ** -0.5),
        'conv_w': nrm(ks[14], (DEPTH, CONV_W, CONV_CH), CONV_W ** -0.5),
        'conv_b': nrm(ks[15], (DEPTH, CONV_CH), 0.02),
        'dt_bias': dt0 + jnp.log(-jnp.expm1(-dt0)),
        'a_log': jnp.log(jax.random.uniform(ks[17], (DEPTH, 2, SSM_HEADS), jnp.float32, minval=1.0, maxval=16.0)),
        'd_skip': 1.0 + nrm(ks[18], (DEPTH, SSM_HEADS), 0.02),
        'ssm_norm_g': 1.0 + nrm(ks[19], (DEPTH, SSM_WIDTH), 0.02),
        'lambda_q1': nrm(ks[20], (DEPTH, ATTN_DK), 0.1),
        'lambda_k1': nrm(ks[21], (DEPTH, ATTN_DK), 0.1),
        'lambda_q2': nrm(ks[22], (DEPTH, ATTN_DK), 0.1),
        'lambda_k2': nrm(ks[23], (DEPTH, ATTN_DK), 0.1),
        'subln_g': 1.0 + nrm(ks[24], (DEPTH, ATTN_DV), 0.02),
        'w_out': nrm(ks[25], (DEPTH, MIX_WIDTH, D_MODEL), BETA * MIX_WIDTH ** -0.5),
    }


def reference(x_prompt, x_sample, c, cache_k, cache_v, state_ssm, c_ctx, w_mod, b_mod, ln_g, ln_b,
              w_ffn_up, w_ffn_down, w_in, conv_w, conv_b, dt_bias, a_log, d_skip, ssm_norm_g,
              lambda_q1, lambda_k1, lambda_q2, lambda_k2, subln_g, w_out):
    p = {'ln_g': ln_g, 'ln_b': ln_b, 'w_ffn_up': w_ffn_up, 'w_ffn_down': w_ffn_down, 'w_in': w_in,
         'conv_w': conv_w, 'conv_b': conv_b, 'dt_bias': dt_bias, 'a_log': a_log, 'd_skip': d_skip,
         'ssm_norm_g': ssm_norm_g, 'lambda_q1': lambda_q1, 'lambda_k1': lambda_k1,
         'lambda_q2': lambda_q2, 'lambda_k2': lambda_k2, 'subln_g': subln_g, 'w_out': w_out}

    y_prompt = x_prompt
    ks_new, vs_new, ss_new = [], [], []
    for l in range(DEPTH):
        mod_ctx = _modulation(c_ctx[None, :], w_mod[l], b_mod[l])
        y_prompt, k_l, v_l, s_l = _trunk_layer(y_prompt, mod_ctx, l, p)
        ks_new.append(k_l)
        vs_new.append(v_l)
        ss_new.append(s_l)

    y_sample = x_sample
    for l in range(DEPTH):
        mod_lat = _modulation(c, w_mod[l], b_mod[l])
        y_sample, _, _, _ = _trunk_layer(y_sample, mod_lat, l, p, cache_k[:, l], cache_v[:, l],
                                         state_ssm[:, l, 0], state_ssm[:, l, 1])

    new_cache_k = jnp.stack(ks_new, axis=1)
    new_cache_v = jnp.stack(vs_new, axis=1)
    new_state_ssm = jnp.stack(ss_new, axis=1)
    return (y_prompt, y_sample, new_cache_k, new_cache_v, new_state_ssm)
```

```python
import functools
import math

import jax
import jax.numpy as jnp
from jax import lax
from jax.experimental import pallas as pl
from jax.experimental.pallas import tpu as pltpu

GRID_W = 64
CHUNK = 128
ROPE_THETA = 10000.0
LN_EPS = 1e-5
RMS_EPS = 1e-6
N_SUB = 3
MOD_ROWS = 8
LANES = 128
SUBLANES = 8
HALO = SUBLANES
VMEM_LIMIT = 56 * 1024 * 1024

F32 = jnp.float32
BF16 = jnp.bfloat16


def _cparams(sem):
    return pltpu.CompilerParams(dimension_semantics=sem, vmem_limit_bytes=VMEM_LIMIT)


def _pick(dim, pref):
    best = None
    t = LANES
    while t <= min(dim, pref):
        if dim % t == 0:
            best = t
        t += LANES
    return best if best is not None else dim


def _sigmoid(x):
    return 1.0 / (1.0 + jnp.exp(-x))


def _mod_kernel(c_ref, w_ref, b_ref, o_ref):
    c = c_ref[...]
    a = (c * _sigmoid(c)).astype(BF16)
    o_ref[...] = jnp.dot(a, w_ref[...].astype(BF16), preferred_element_type=F32) + b_ref[...]


def _modulation(cvec, w_mod, b_mod):
    depth, d, n = w_mod.shape
    tn = _pick(n, 512)
    out = pl.pallas_call(
        _mod_kernel,
        out_shape=jax.ShapeDtypeStruct((depth, MOD_ROWS, n), F32),
        grid=(depth, n // tn),
        in_specs=[pl.BlockSpec((MOD_ROWS, d), lambda l, j: (0, 0)),
                  pl.BlockSpec((None, d, tn), lambda l, j: (l, 0, j)),
                  pl.BlockSpec((None, 1, tn), lambda l, j: (l, 0, j))],
        out_specs=pl.BlockSpec((None, MOD_ROWS, tn), lambda l, j: (l, 0, j)),
        compiler_params=_cparams(("arbitrary", "arbitrary")),
        name="modulation",
    )(cvec, w_mod, b_mod.reshape(depth, 1, n))
    return out.reshape(depth, MOD_ROWS, N_SUB * 3, d)


def _prep_kernel(x_ref, m_ref, h_ref):
    shift = m_ref[0:1, :]
    scale = m_ref[1:2, :]
    h_ref[...] = (x_ref[...] * (1.0 + scale) + shift).astype(BF16)


def _prep(x, mod_l, row_of_tile, tm):
    m, d = x.shape
    return pl.pallas_call(
        _prep_kernel,
        out_shape=jax.ShapeDtypeStruct((m, d), BF16),
        grid=(m // tm,),
        in_specs=[pl.BlockSpec((tm, d), lambda i: (i, 0)),
                  pl.BlockSpec((None, N_SUB * 3, d), lambda i: (row_of_tile(i, tm), 0, 0))],
        out_specs=pl.BlockSpec((tm, d), lambda i: (i, 0)),
        compiler_params=_cparams(("arbitrary",)),
        name="prep",
    )(x, mod_l)


def _up_kernel(h_ref, wg_ref, wu_ref, o_ref):
    h = h_ref[...]
    g = jnp.dot(h, wg_ref[...], preferred_element_type=F32)
    u = jnp.dot(h, wu_ref[...], preferred_element_type=F32)
    o_ref[...] = (g * _sigmoid(g) * u).astype(o_ref.dtype)


def _ffn_up(h, w_up, l, s, fp):
    m, d = h.shape
    tm = _pick(m, 1024)
    tn = _pick(fp, 512)
    nj = fp // tn
    return pl.pallas_call(
        _up_kernel,
        out_shape=jax.ShapeDtypeStruct((m, fp), BF16),
        grid=(m // tm, nj),
        in_specs=[pl.BlockSpec((tm, d), lambda i, j: (i, 0)),
                  pl.BlockSpec((None, None, d, tn), lambda i, j: (l, s, 0, j)),
                  pl.BlockSpec((None, None, d, tn), lambda i, j: (l, s, 0, j + nj))],
        out_specs=pl.BlockSpec((tm, tn), lambda i, j: (i, j)),
        compiler_params=_cparams(("arbitrary", "arbitrary")),
        name="ffn_up",
    )(h, w_up, w_up)


def _mm_ln_kernel(*refs, nk, sub, coef, alpha, has_next, rows):
    if has_next:
        a_ref, w_ref, x_ref, mc_ref, mn_ref, g_ref, b_ref, xo_ref, ho_ref = refs
    else:
        a_ref, w_ref, x_ref, mc_ref, g_ref, b_ref, xo_ref = refs
    k = pl.program_id(1)
    d = xo_ref.shape[1]
    slab = _pick(d, 512)

    def partial_products(first):
        a = a_ref[...]
        for n in range(d // slab):
            cols = slice(n * slab, (n + 1) * slab)
            p = jnp.dot(a, w_ref[:, cols], preferred_element_type=F32)
            if first:
                xo_ref[:, cols] = p
            else:
                xo_ref[:, cols] += p

    @pl.when(k == 0)
    def _():
        partial_products(True)

    @pl.when(k > 0)
    def _():
        partial_products(False)

    @pl.when(k == nk - 1)
    def _():
        gate = mc_ref[3 * sub + 2:3 * sub + 3, :] * coef
        ln_g = g_ref[...]
        ln_b = b_ref[...]
        if has_next:
            nsub = (sub + 1) % N_SUB
            shift_n = mn_ref[3 * nsub:3 * nsub + 1, :]
            scale_n = 1.0 + mn_ref[3 * nsub + 1:3 * nsub + 2, :]

        def body(r, carry):
            sl = pl.ds(pl.multiple_of(r * rows, rows), rows)
            y = alpha * x_ref[sl, :] + gate * xo_ref[sl, :]
            mu = jnp.mean(y, axis=-1, keepdims=True)
            yc = y - mu
            var = jnp.mean(yc * yc, axis=-1, keepdims=True)
            xn = yc * lax.rsqrt(var + LN_EPS) * ln_g + ln_b
            xo_ref[sl, :] = xn
            if has_next:
                ho_ref[sl, :] = (xn * scale_n + shift_n).astype(BF16)
            return carry

        lax.fori_loop(0, xo_ref.shape[0] // rows, body, 0)


def _mm_ln(a, w, w_idx, x, mod_cur, mod_next, ln_g, ln_b, row_of_tile, *, sub, coef, alpha):
    m, kdim = a.shape
    d = x.shape[1]
    tm = _pick(m, 512)
    tk = _pick(kdim, 512)
    nk = kdim // tk
    has_next = mod_next is not None
    nlead = len(w_idx)
    w_spec = pl.BlockSpec((None,) * nlead + (tk, d), lambda i, k: tuple(w_idx) + (k, 0))
    mod_spec = pl.BlockSpec((None, N_SUB * 3, d), lambda i, k: (row_of_tile(i, tm), 0, 0))
    vec_spec = pl.BlockSpec((1, d), lambda i, k: (0, 0))
    tile_spec = pl.BlockSpec((tm, d), lambda i, k: (i, 0))
    in_specs = [pl.BlockSpec((tm, tk), lambda i, k: (i, k)), w_spec, tile_spec, mod_spec]
    args = [a, w, x, mod_cur]
    if has_next:
        in_specs.append(mod_spec)
        args.append(mod_next)
    in_specs += [vec_spec, vec_spec]
    args += [ln_g.reshape(1, d), ln_b.reshape(1, d)]
    out_shape = [jax.ShapeDtypeStruct((m, d), F32)]
    out_specs = [tile_spec]
    if has_next:
        out_shape.append(jax.ShapeDtypeStruct((m, d), BF16))
        out_specs.append(tile_spec)
    res = pl.pallas_call(
        functools.partial(_mm_ln_kernel, nk=nk, sub=sub, coef=coef, alpha=alpha, has_next=has_next,
                          rows=SUBLANES),
        out_shape=out_shape,
        grid=(m // tm, nk),
        in_specs=in_specs,
        out_specs=out_specs,
        compiler_params=_cparams(("arbitrary", "arbitrary")),
        name="mm_ln",
    )(*args)
    return (res[0], res[1]) if has_next else (res[0], None)


def _proj_kernel(*refs, mode, scale, n_scaled):
    if mode == "rope":
        h_ref, w_ref, c_ref, sl_ref, sr_ref, o_ref = refs
    else:
        h_ref, w_ref, o_ref = refs
    acc = jnp.dot(h_ref[...], w_ref[...], preferred_element_type=F32)
    if mode == "f32":
        o_ref[...] = acc
    elif mode == "bf16":
        o_ref[...] = (acc * scale).astype(BF16)
    else:
        sc = jnp.where(pl.program_id(1) < n_scaled, scale, 1.0)
        cos = c_ref[...]
        s_lo = sl_ref[...]
        s_hi = sr_ref[...]
        for t in range(acc.shape[1] // LANES):
            x = acc[:, t * LANES:(t + 1) * LANES]
            r = x * cos + pltpu.roll(x, LANES - 16, 1) * s_lo + pltpu.roll(x, 16, 1) * s_hi
            o_ref[:, t * LANES:(t + 1) * LANES] = (r * sc).astype(BF16)


def _proj(h, w_in, l, *, row0, rows, col0, cols, mode, scale=1.0, n_scaled_cols=0, tables=None, seq=None):
    d = h.shape[1]
    tm = _pick(math.gcd(rows, row0), 1024)
    tn = _pick(math.gcd(math.gcd(cols, col0), n_scaled_cols), 512)
    i0, j0 = row0 // tm, col0 // tn
    in_specs = [pl.BlockSpec((tm, d), lambda i, j: (i + i0, 0)),
                pl.BlockSpec((None, d, tn), lambda i, j: (l, 0, j + j0))]
    args = [h, w_in]
    if mode == "rope":
        tm = _pick(math.gcd(seq, tm), tm)
        i0 = row0 // tm
        in_specs[0] = pl.BlockSpec((tm, d), lambda i, j: (i + i0, 0))
        nt = seq // tm
        tab_spec = pl.BlockSpec((tm, LANES), lambda i, j: (i % nt, 0))
        in_specs += [tab_spec] * 3
        args += list(tables)
    out_dtype = F32 if mode == "f32" else BF16
    return pl.pallas_call(
        functools.partial(_proj_kernel, mode=mode, scale=scale, n_scaled=n_scaled_cols // tn),
        out_shape=jax.ShapeDtypeStruct((rows, cols), out_dtype),
        grid=(rows // tm, cols // tn),
        in_specs=in_specs,
        out_specs=pl.BlockSpec((tm, tn), lambda i, j: (i, j)),
        compiler_params=_cparams(("arbitrary", "arbitrary")),
        name="proj_" + mode,
    )(*args)


def _rope_tables(seq, dk):
    n_freq = dk // 4
    pos = jnp.arange(seq, dtype=jnp.int32)
    row = (pos // GRID_W).astype(F32)
    col = (pos % GRID_W).astype(F32)
    inv_freq = ROPE_THETA ** (-jnp.arange(n_freq, dtype=F32) / n_freq)
    ang_r = row[:, None] * inv_freq
    ang_c = col[:, None] * inv_freq
    zeros = jnp.zeros_like(ang_r)

    def lanes(a_r, a_c):
        one_map = jnp.concatenate([a_r[0], a_r[1], a_c[0], a_c[1]], axis=-1)
        return jnp.concatenate([one_map] * (LANES // dk), axis=-1)

    cos = lanes((jnp.cos(ang_r),) * 2, (jnp.cos(ang_c),) * 2)
    s_lo = lanes((-jnp.sin(ang_r), zeros), (-jnp.sin(ang_c), zeros))
    s_hi = lanes((zeros, jnp.sin(ang_r)), (zeros, jnp.sin(ang_c)))
    return cos, s_lo, s_hi


def _lambda(lam_ref, lam_init):
    lq1 = lam_ref[0:1, :]
    lk1 = lam_ref[1:2, :]
    lq2 = lam_ref[2:3, :]
    lk2 = lam_ref[3:4, :]
    return (jnp.exp(jnp.sum(lq1 * lk1, axis=-1, keepdims=True))
            - jnp.exp(jnp.sum(lq2 * lk2, axis=-1, keepdims=True)) + lam_init)


_NT = (((1,), (1,)), ((), ()))


def _head_attention(q, keys, vals, lam, gain, lam_init, dk):
    lane = lax.broadcasted_iota(jnp.int32, q.shape, 1)
    qf = q.astype(F32)
    qm = (jnp.where(lane < dk, qf, 0.0).astype(BF16), jnp.where(lane >= dk, qf, 0.0).astype(BF16))
    probs = []
    for qi in qm:
        s = [lax.dot_general(qi, k, _NT, preferred_element_type=F32) for k in keys]
        mx = functools.reduce(jnp.maximum, [jnp.max(si, axis=-1, keepdims=True) for si in s])
        e = [jnp.exp(si - mx) for si in s]
        den = functools.reduce(lambda a, b: a + b, [jnp.sum(ei, axis=-1, keepdims=True) for ei in e])
        inv = 1.0 / den
        probs.append([ei * inv for ei in e])
    o = None
    for p0, p1, v in zip(probs[0], probs[1], vals):
        a = (p0 - lam * p1).astype(BF16)
        t = jnp.dot(a, v, preferred_element_type=F32)
        o = t if o is None else o + t
    ms = jnp.mean(o * o, axis=-1, keepdims=True)
    return o * lax.rsqrt(ms + RMS_EPS) * gain * (1.0 - lam_init)


def _attn_ctx_kernel(q_ref, k_ref, v_ref, lam_ref, g_ref, o_ref, *, hb, lam_init, dk):
    lam = _lambda(lam_ref, lam_init)
    gain = g_ref[...]
    for hh in range(hb):
        sl = slice(hh * LANES, (hh + 1) * LANES)
        y = _head_attention(q_ref[:, sl], [k_ref[:, sl].astype(BF16)], [v_ref[:, sl].astype(BF16)],
                            lam, gain, lam_init, dk)
        o_ref[:, sl] = y.astype(BF16)


def _attn_lat_kernel(q_ref, kl_ref, vl_ref, kc_ref, vc_ref, lam_ref, g_ref, o_ref, *, hb, lam_init, dk):
    lam = _lambda(lam_ref, lam_init)
    gain = g_ref[...]
    for hh in range(hb):
        sl = slice(hh * LANES, (hh + 1) * LANES)
        y = _head_attention(q_ref[:, sl],
                            [kc_ref[:, sl].astype(BF16), kl_ref[:, sl]],
                            [vc_ref[:, sl].astype(BF16), vl_ref[:, sl]],
                            lam, gain, lam_init, dk)
        o_ref[:, sl] = y.astype(BF16)


def _attention_ctx(q, kv, lam_p, gain, *, batch, seq, heads, lam_init, dk):
    hb = min(heads, 4)
    w = hb * LANES
    nh = heads // hb
    return pl.pallas_call(
        functools.partial(_attn_ctx_kernel, hb=hb, lam_init=lam_init, dk=dk),
        out_shape=jax.ShapeDtypeStruct((batch * seq, heads * LANES), BF16),
        grid=(batch, nh),
        in_specs=[pl.BlockSpec((seq, w), lambda b, h: (b, h)),
                  pl.BlockSpec((seq, w), lambda b, h: (b, h)),
                  pl.BlockSpec((seq, w), lambda b, h: (b, nh + h)),
                  pl.BlockSpec(lam_p.shape, lambda b, h: (0, 0)),
                  pl.BlockSpec((1, LANES), lambda b, h: (0, 0))],
        out_specs=pl.BlockSpec((seq, w), lambda b, h: (b, h)),
        compiler_params=_cparams(("arbitrary", "arbitrary")),
        name="attn_ctx",
    )(q, kv, kv, lam_p, gain)


def _attention_lat(qk, v, cache_k, cache_v, l, lam_p, gain, *, batch, seq, heads, lam_init, dk):
    hb = 1
    w = hb * LANES
    nh = heads // hb
    tq = _pick(seq, 256)
    nq = seq // tq
    past = cache_k.shape[2]
    return pl.pallas_call(
        functools.partial(_attn_lat_kernel, hb=hb, lam_init=lam_init, dk=dk),
        out_shape=jax.ShapeDtypeStruct((batch * seq, heads * LANES), BF16),
        grid=(batch, nh, nq),
        in_specs=[pl.BlockSpec((tq, w), lambda b, h, i: (b * nq + i, h)),
                  pl.BlockSpec((seq, w), lambda b, h, i: (b, nh + h)),
                  pl.BlockSpec((seq, w), lambda b, h, i: (b, h)),
                  pl.BlockSpec((None, None, past, w), lambda b, h, i: (b, l, 0, h)),
                  pl.BlockSpec((None, None, past, w), lambda b, h, i: (b, l, 0, h)),
                  pl.BlockSpec(lam_p.shape, lambda b, h, i: (0, 0)),
                  pl.BlockSpec((1, LANES), lambda b, h, i: (0, 0))],
        out_specs=pl.BlockSpec((tq, w), lambda b, h, i: (b * nq + i, h)),
        compiler_params=_cparams(("arbitrary", "arbitrary", "arbitrary")),
        name="attn_lat",
    )(qk, qk, v, cache_k, cache_v, lam_p, gain)


def _conv_kernel(xc_ref, xp_ref, xn_ref, w_ref, b_ref, o_ref, buf, *, conv_w, n_ctx_chunks, cpc, cpl):
    i = pl.program_id(0)
    in_ctx = i < n_ctx_chunks
    pos = jnp.where(in_ctx, i % cpc, (i - n_ctx_chunks) % cpl)
    last = jnp.where(in_ctx, cpc - 1, cpl - 1)
    prev = xp_ref[...]
    nxt = xn_ref[...]
    buf[0:HALO, :] = jnp.where(pos == 0, jnp.zeros_like(prev), prev)
    buf[HALO:HALO + CHUNK, :] = xc_ref[...]
    buf[HALO + CHUNK:, :] = jnp.where(pos == last, jnp.zeros_like(nxt), nxt)
    pad = (conv_w - 1) // 2
    acc = None
    for k in range(conv_w):
        term = buf[HALO - pad + k:HALO - pad + k + CHUNK, :] * w_ref[k:k + 1, :]
        acc = term if acc is None else acc + term
    acc = acc + b_ref[...]
    o_ref[...] = acc * _sigmoid(acc)


def _conv_silu(zx, conv_w_l, conv_b_l, *, conv_ch, n_ctx, seq_ctx, seq_lat):
    m = zx.shape[0]
    kw = conv_w_l.shape[0]
    nb = m // HALO
    per = CHUNK // HALO
    return pl.pallas_call(
        functools.partial(_conv_kernel, conv_w=kw, n_ctx_chunks=n_ctx // CHUNK,
                          cpc=seq_ctx // CHUNK, cpl=seq_lat // CHUNK),
        out_shape=jax.ShapeDtypeStruct((m, conv_ch), F32),
        grid=(m // CHUNK,),
        in_specs=[pl.BlockSpec((CHUNK, conv_ch), lambda i: (i, 0)),
                  pl.BlockSpec((HALO, conv_ch), lambda i: (jnp.maximum(i * per - 1, 0), 0)),
                  pl.BlockSpec((HALO, conv_ch), lambda i: (jnp.minimum((i + 1) * per, nb - 1), 0)),
                  pl.BlockSpec((kw, conv_ch), lambda i: (0, 0)),
                  pl.BlockSpec((1, conv_ch), lambda i: (0, 0))],
        out_specs=pl.BlockSpec((CHUNK, conv_ch), lambda i: (i, 0)),
        scratch_shapes=[pltpu.VMEM((CHUNK + 2 * HALO, conv_ch), F32)],
        compiler_params=_cparams(("arbitrary",)),
        name="conv_silu",
    )(zx, zx, zx, conv_w_l, conv_b_l.reshape(1, conv_ch))


def _split3(x):
    hi = x.astype(BF16)
    r1 = x - hi.astype(F32)
    mid = r1.astype(BF16)
    lo = (r1 - mid.astype(F32)).astype(BF16)
    return hi, mid, lo


def _tri_sum(tri, x):
    hi, mid, lo = _split3(x)
    return (jnp.dot(tri, hi, preferred_element_type=F32) + jnp.dot(tri, mid, preferred_element_type=F32)
            + jnp.dot(tri, lo, preferred_element_type=F32))


def _colb(x, j):
    return jnp.broadcast_to(x[:, j:j + 1], (x.shape[0], LANES))


def _scan_dir(xa_ref, dt_ref, bias, a_neg, h_ref, y_ref, *, backward, n_heads, hpg, pdim, ssm_w, bc_cols):
    q = CHUNK
    ri = lax.broadcasted_iota(jnp.int32, (q, q), 0)
    ci = lax.broadcasted_iota(jnp.int32, (q, q), 1)
    keep = (ci >= ri) if backward else (ri >= ci)
    tri = jnp.where(keep, 1.0, 0.0).astype(BF16)
    lane = lax.broadcasted_iota(jnp.int32, (q, LANES), 1)
    first_half = lane < pdim

    raw = dt_ref[...] + bias
    dt = jnp.maximum(raw, 0.0) + jnp.log(1.0 + jnp.exp(-jnp.abs(raw)))
    acum = _tri_sum(tri, dt * a_neg)
    edge = acum[0:1, :] if backward else acum[q - 1:q, :]
    wend = jnp.exp(edge - acum) * dt
    eac = jnp.exp(acum)
    cdec = jnp.exp(edge)
    acum_t = acum.T
    dt_t = dt.T

    lane0 = n_heads if backward else 0
    heads_per_tile = LANES // pdim
    gmat = {}
    for t in range(n_heads // heads_per_tile):
        xs = xa_ref[:, t * LANES:(t + 1) * LANES]
        xs_b = xs.astype(BF16)
        y_intra = []
        w_cols, e_cols, d_cols = [], [], []
        grp = (t * heads_per_tile) // hpg
        if grp not in gmat:
            bm = xa_ref[:, ssm_w + grp * LANES:ssm_w + (grp + 1) * LANES].astype(BF16)
            cm = xa_ref[:, ssm_w + bc_cols + grp * LANES:ssm_w + bc_cols + (grp + 1) * LANES].astype(BF16)
            gmat = {grp: (bm, cm, lax.dot_general(cm, bm, _NT, preferred_element_type=F32))}
        bm, cm, g = gmat[grp]
        for hh in range(heads_per_tile):
            j = lane0 + t * heads_per_tile + hh
            seg = _colb(acum, j) - acum_t[j:j + 1, :]
            decay = jnp.exp(jnp.where(keep, seg, -jnp.inf))
            scores = (g * decay * dt_t[j:j + 1, :]).astype(BF16)
            y_intra.append(jnp.dot(scores, xs_b, preferred_element_type=F32))
            w_cols.append(_colb(wend, j))
            e_cols.append(_colb(eac, j))
            d_cols.append(jnp.broadcast_to(cdec[:, j:j + 1], (pdim, LANES)))
        if heads_per_tile == 2:
            yi = jnp.where(first_half, y_intra[0], y_intra[1])
            wp = jnp.where(first_half, w_cols[0], w_cols[1])
            ep = jnp.where(first_half, e_cols[0], e_cols[1])
        else:
            yi, wp, ep = y_intra[0], w_cols[0], e_cols[0]
        dp = jnp.concatenate(d_cols, axis=0)
        rows = slice(t * LANES, (t + 1) * LANES)
        h_old = h_ref[rows, :]
        y_inter = lax.dot_general(cm, h_old.astype(BF16), _NT, preferred_element_type=F32) * ep
        y_ref[:, t * LANES:(t + 1) * LANES] = yi + y_inter
        s_chunk = lax.dot_general((xs * wp).astype(BF16), bm, (((0,), (0,)), ((), ())),
                                  preferred_element_type=F32)
        h_ref[rows, :] = dp * h_old + s_chunk


def _scan_kernel(*refs, zero_init, nc, **dims):
    if zero_init:
        xf_ref, xb_ref, dtf_ref, dtb_ref, bias_ref, alog_ref, yf_ref, yb_ref, hout_ref, hf, hb = refs
    else:
        xf_ref, xb_ref, dtf_ref, dtb_ref, bias_ref, alog_ref, h0_ref, yf_ref, yb_ref, hout_ref, hf, hb = refs
    c = pl.program_id(1)

    @pl.when(c == 0)
    def _():
        if zero_init:
            hf[...] = jnp.zeros_like(hf)
            hb[...] = jnp.zeros_like(hb)
        else:
            hf[...] = h0_ref[0]
            hb[...] = h0_ref[1]

    bias = bias_ref[...]
    a_neg = -jnp.exp(alog_ref[...])
    _scan_dir(xf_ref, dtf_ref, bias, a_neg, hf, yf_ref, backward=False, **dims)
    _scan_dir(xb_ref, dtb_ref, bias, a_neg, hb, yb_ref, backward=True, **dims)

    @pl.when(c == nc - 1)
    def _():
        hout_ref[0] = hf[...]
        hout_ref[1] = hb[...]


def _ssd_scan(xact, dt, bias, alog, h0, l, *, row0, batch, seq, n_heads, hpg, pdim, nstate, ssm_w, bc_cols):
    nc = seq // CHUNK
    c0 = row0 // CHUNK
    conv_ch = xact.shape[1]
    hp = n_heads * pdim
    zero_init = h0 is None
    fwd = lambda b, c: (c0 + b * nc + c, 0)
    bwd = lambda b, c: (c0 + b * nc + (nc - 1 - c), 0)
    vec = pl.BlockSpec((1, LANES), lambda b, c: (0, 0))
    in_specs = [pl.BlockSpec((CHUNK, conv_ch), fwd), pl.BlockSpec((CHUNK, conv_ch), bwd),
                pl.BlockSpec((CHUNK, LANES), fwd), pl.BlockSpec((CHUNK, LANES), bwd), vec, vec]
    args = [xact, xact, dt, dt, bias, alog]
    if not zero_init:
        in_specs.append(pl.BlockSpec((None, None, 2, hp, nstate), lambda b, c: (b, l, 0, 0, 0)))
        args.append(h0)
    yfwd = lambda b, c: (b * nc + c, 0)
    ybwd = lambda b, c: (b * nc + (nc - 1 - c), 0)
    dims = dict(n_heads=n_heads, hpg=hpg, pdim=pdim, ssm_w=ssm_w, bc_cols=bc_cols)
    return pl.pallas_call(
        functools.partial(_scan_kernel, zero_init=zero_init, nc=nc, **dims),
        out_shape=[jax.ShapeDtypeStruct((batch * seq, ssm_w), F32),
                   jax.ShapeDtypeStruct((batch * seq, ssm_w), F32),
                   jax.ShapeDtypeStruct((batch, 2, hp, nstate), F32)],
        grid=(batch, nc),
        in_specs=in_specs,
        out_specs=[pl.BlockSpec((CHUNK, ssm_w), yfwd), pl.BlockSpec((CHUNK, ssm_w), ybwd),
                   pl.BlockSpec((None, 2, hp, nstate), lambda b, c: (b, 0, 0, 0))],
        scratch_shapes=[pltpu.VMEM((hp, nstate), F32), pltpu.VMEM((hp, nstate), F32)],
        compiler_params=_cparams(("arbitrary", "arbitrary")),
        name="ssd_scan",
    )(*args)


def _ssd_out_kernel(yf_ref, yb_ref, xs_ref, z_ref, d_ref, g_ref, o_ref):
    y = yf_ref[...] + yb_ref[...] + d_ref[...] * xs_ref[...]
    z = z_ref[...]
    t = y * (z * _sigmoid(z))
    ms = jnp.mean(t * t, axis=-1, keepdims=True)
    o_ref[...] = (t * lax.rsqrt(ms + RMS_EPS) * g_ref[...]).astype(BF16)


def _ssd_out(yf, yb, xact, zx, d_lanes, gain, *, ssm_w, conv_ch):
    m = yf.shape[0]
    tm = _pick(m, 256)
    zblk = conv_ch // ssm_w
    tile = pl.BlockSpec((tm, ssm_w), lambda i: (i, 0))
    vec = pl.BlockSpec((1, ssm_w), lambda i: (0, 0))
    return pl.pallas_call(
        _ssd_out_kernel,
        out_shape=jax.ShapeDtypeStruct((m, ssm_w), BF16),
        grid=(m // tm,),
        in_specs=[tile, tile, tile, pl.BlockSpec((tm, ssm_w), lambda i: (i, zblk)), vec, vec],
        out_specs=tile,
        compiler_params=_cparams(("arbitrary",)),
        name="ssd_out",
    )(yf, yb, xact, zx, d_lanes, gain)


def kernel(x_prompt, x_sample, c, cache_k, cache_v, state_ssm, c_ctx, w_mod, b_mod, ln_g, ln_b, w_ffn_up, w_ffn_down, w_in, conv_w, conv_b, dt_bias, a_log, d_skip, ssm_norm_g, lambda_q1, lambda_k1, lambda_q2, lambda_k2, subln_g, w_out):
    batch, seq, d = x_prompt.shape
    dbatch, dseq, _ = x_sample.shape
    depth = w_in.shape[0]
    heads, dk = cache_k.shape[3], cache_k.shape[5]
    dv = cache_v.shape[4]
    assert dv == LANES and 2 * dk == LANES, "one attention head must fill one 128-lane tile"
    attn_w = heads * dv
    n_heads, pdim, nstate = state_ssm.shape[3], state_ssm.shape[4], state_ssm.shape[5]
    assert nstate == LANES and LANES % pdim == 0 and 2 * n_heads <= LANES
    ssm_w = n_heads * pdim
    conv_ch = conv_w.shape[2]
    bc_cols = (conv_ch - ssm_w) // 2
    hpg = n_heads // (bc_cols // nstate)
    assert hpg % (LANES // pdim) == 0
    dff = w_ffn_down.shape[2]
    fp = -(-dff // 512) * 512
    alpha = (2 * depth) ** 0.25
    n_ctx, n_lat = batch * seq, dbatch * dseq
    assert seq % CHUNK == 0 and dseq % CHUNK == 0 and dseq % GRID_W == 0
    dt_cols = 2 * n_heads

    def row_of_tile(i, tm):
        return jnp.where(i * tm < n_ctx, 0, 1 + (i * tm - n_ctx) // dseq)

    pad_f = fp - dff
    w_up = jnp.concatenate(
        [jnp.pad(w_ffn_up[..., :dff], ((0, 0),) * 3 + ((0, pad_f),)),
         jnp.pad(w_ffn_up[..., dff:], ((0, 0),) * 3 + ((0, pad_f),))], axis=-1).astype(BF16)
    w_down = jnp.pad(w_ffn_down, ((0, 0), (0, 0), (0, pad_f), (0, 0))).astype(BF16)
    o_z = 3 * attn_w
    o_x = o_z + ssm_w
    o_dt = o_x + conv_ch
    w_in_b = jnp.concatenate(
        [w_in[..., :o_z], w_in[..., o_x:o_dt], w_in[..., o_z:o_x],
         jnp.pad(w_in[..., o_dt:], ((0, 0), (0, 0), (0, LANES - dt_cols)))], axis=-1).astype(BF16)
    w_out_b = w_out.astype(BF16)
    col_x = 3 * attn_w
    col_dt = col_x + conv_ch + ssm_w

    cvec = jnp.concatenate([c_ctx[None, :], c, jnp.zeros((MOD_ROWS - 1 - dbatch, d), F32)], axis=0)
    mod = _modulation(cvec, w_mod, b_mod)

    x = jnp.concatenate([x_prompt.reshape(n_ctx, d), x_sample.reshape(n_lat, d)], axis=0)
    tables = _rope_tables(dseq, dk)
    ck = cache_k.reshape(dbatch, depth, cache_k.shape[2], attn_w)
    cv = cache_v.reshape(dbatch, depth, cache_v.shape[2], attn_w)
    h0_all = state_ssm.reshape(dbatch, depth, 2, ssm_w, nstate)
    scale = dk ** -0.5

    h = _prep(x, mod[0], row_of_tile, _pick(n_ctx, 512))
    ks_new, vs_new, ss_new = [], [], []
    for l in range(depth):
        lam_init = 0.8 - 0.6 * math.exp(-0.3 * l)
        lam_p = jnp.stack([lambda_q1[l], lambda_k1[l], lambda_q2[l], lambda_k2[l]], axis=0)
        pad_l = ((0, 0), (0, LANES - dt_cols))
        bias_l = jnp.pad(dt_bias[l].reshape(1, dt_cols), pad_l)
        alog_l = jnp.pad(a_log[l].reshape(1, dt_cols), pad_l)
        d_lanes = jnp.repeat(d_skip[l], pdim).reshape(1, ssm_w)

        act = _ffn_up(h, w_up, l, 0, fp)
        x, h = _mm_ln(act, w_down, (l, 0), x, mod[l], mod[l], ln_g[l, 0], ln_b[l, 0], row_of_tile,
                      sub=0, coef=0.5, alpha=alpha)

        zx = _proj(h, w_in_b, l, row0=0, rows=n_ctx + n_lat, col0=col_x, cols=conv_ch + ssm_w, mode="f32")
        dt = _proj(h, w_in_b, l, row0=0, rows=n_ctx + n_lat, col0=col_dt, cols=LANES, mode="f32")
        q_ctx = _proj(h, w_in_b, l, row0=0, rows=n_ctx, col0=0, cols=attn_w, mode="bf16", scale=scale)
        kv_ctx = _proj(h, w_in_b, l, row0=0, rows=n_ctx, col0=attn_w, cols=2 * attn_w, mode="f32")
        qk_lat = _proj(h, w_in_b, l, row0=n_ctx, rows=n_lat, col0=0, cols=2 * attn_w, mode="rope",
                       scale=scale, n_scaled_cols=attn_w, tables=tables, seq=dseq)
        v_lat = _proj(h, w_in_b, l, row0=n_ctx, rows=n_lat, col0=2 * attn_w, cols=attn_w, mode="bf16")

        gain = subln_g[l].reshape(1, dv)
        a_ctx = _attention_ctx(q_ctx, kv_ctx, lam_p, gain, batch=batch, seq=seq, heads=heads,
                               lam_init=lam_init, dk=dk)
        a_lat = _attention_lat(qk_lat, v_lat, ck, cv, l, lam_p, gain, batch=dbatch, seq=dseq, heads=heads,
                               lam_init=lam_init, dk=dk)

        xact = _conv_silu(zx, conv_w[l], conv_b[l], conv_ch=conv_ch, n_ctx=n_ctx, seq_ctx=seq, seq_lat=dseq)
        sdims = dict(n_heads=n_heads, hpg=hpg, pdim=pdim, nstate=nstate, ssm_w=ssm_w, bc_cols=bc_cols)
        yf_c, yb_c, s_ctx = _ssd_scan(xact, dt, bias_l, alog_l, None, l, row0=0, batch=batch, seq=seq, **sdims)
        yf_l, yb_l, _ = _ssd_scan(xact, dt, bias_l, alog_l, h0_all, l, row0=n_ctx, batch=dbatch, seq=dseq,
                                  **sdims)
        ssd = _ssd_out(jnp.concatenate([yf_c, yf_l], axis=0), jnp.concatenate([yb_c, yb_l], axis=0),
                       xact, zx, d_lanes, ssm_norm_g[l].reshape(1, ssm_w), ssm_w=ssm_w, conv_ch=conv_ch)

        mix = jnp.concatenate([jnp.concatenate([a_ctx, a_lat], axis=0), ssd], axis=1)
        x, h = _mm_ln(mix, w_out_b, (l,), x, mod[l], mod[l], ln_g[l, 1], ln_b[l, 1], row_of_tile,
                      sub=1, coef=1.0, alpha=alpha)

        act = _ffn_up(h, w_up, l, 1, fp)
        mod_next = mod[l + 1] if l + 1 < depth else None
        x, h = _mm_ln(act, w_down, (l, 1), x, mod[l], mod_next, ln_g[l, 2], ln_b[l, 2], row_of_tile,
                      sub=2, coef=0.5, alpha=alpha)

        ks_new.append(kv_ctx[:, :attn_w].reshape(batch, seq, heads, 2, dk))
        vs_new.append(kv_ctx[:, attn_w:].reshape(batch, seq, heads, dv))
        ss_new.append(s_ctx.reshape(batch, 2, n_heads, pdim, nstate))

    y_prompt = x[:n_ctx].reshape(batch, seq, d)
    y_sample = x[n_ctx:].reshape(dbatch, dseq, d)
    return (y_prompt, y_sample, jnp.stack(ks_new, axis=1), jnp.stack(vs_new, axis=1),
            jnp.stack(ss_new, axis=1))
```

```python
import functools
import math

import jax
import jax.numpy as jnp
from jax import lax
from jax.experimental import pallas as pl
from jax.experimental.pallas import tpu as pltpu

GRID_W = 64
CHUNK = 128
ROPE_THETA = 10000.0
LN_EPS = 1e-5
RMS_EPS = 1e-6
N_SUB = 3
MOD_ROWS = 8
LANES = 128
SUBLANES = 8
HALO = SUBLANES
LN_ROWS = 32
VMEM_LIMIT = 56 * 1024 * 1024

F32 = jnp.float32
BF16 = jnp.bfloat16


def _cparams(sem):
    return pltpu.CompilerParams(dimension_semantics=sem, vmem_limit_bytes=VMEM_LIMIT)


def _pick(dim, pref):
    best = None
    t = LANES
    while t <= min(dim, pref):
        if dim % t == 0:
            best = t
        t += LANES
    return best if best is not None else dim


def _sigmoid(x):
    return 1.0 / (1.0 + jnp.exp(-x))


def _mod_kernel(c_ref, w_ref, b_ref, o_ref):
    c = c_ref[...]
    a = (c * _sigmoid(c)).astype(BF16)
    o_ref[...] = jnp.dot(a, w_ref[...].astype(BF16), preferred_element_type=F32) + b_ref[...]


def _modulation(cvec, w_mod, b_mod):
    depth, d, n = w_mod.shape
    tn = _pick(n, 512)
    out = pl.pallas_call(
        _mod_kernel,
        out_shape=jax.ShapeDtypeStruct((depth, MOD_ROWS, n), F32),
        grid=(depth, n // tn),
        in_specs=[pl.BlockSpec((MOD_ROWS, d), lambda l, j: (0, 0)),
                  pl.BlockSpec((None, d, tn), lambda l, j: (l, 0, j)),
                  pl.BlockSpec((None, 1, tn), lambda l, j: (l, 0, j))],
        out_specs=pl.BlockSpec((None, MOD_ROWS, tn), lambda l, j: (l, 0, j)),
        compiler_params=_cparams(("arbitrary", "arbitrary")),
        name="modulation",
    )(cvec, w_mod, b_mod.reshape(depth, 1, n))
    return out.reshape(depth, MOD_ROWS, N_SUB * 3, d)


def _prep_kernel(x_ref, m_ref, h_ref):
    shift = m_ref[0:1, :]
    scale = m_ref[1:2, :]
    h_ref[...] = (x_ref[...] * (1.0 + scale) + shift).astype(BF16)


def _prep(x, mod_l, row_of_tile, tm):
    m, d = x.shape
    return pl.pallas_call(
        _prep_kernel,
        out_shape=jax.ShapeDtypeStruct((m, d), BF16),
        grid=(m // tm,),
        in_specs=[pl.BlockSpec((tm, d), lambda i: (i, 0)),
                  pl.BlockSpec((None, N_SUB * 3, d), lambda i: (row_of_tile(i, tm), 0, 0))],
        out_specs=pl.BlockSpec((tm, d), lambda i: (i, 0)),
        compiler_params=_cparams(("arbitrary",)),
        name="prep",
    )(x, mod_l)


def _up_kernel(h_ref, wg_ref, wu_ref, o_ref):
    h = h_ref[...]
    g = jnp.dot(h, wg_ref[...], preferred_element_type=F32)
    u = jnp.dot(h, wu_ref[...], preferred_element_type=F32)
    o_ref[...] = (g * _sigmoid(g) * u).astype(o_ref.dtype)


def _ffn_up(h, w_up, l, s, fp):
    m, d = h.shape
    tm = _pick(m, 1024)
    tn = _pick(fp, 512)
    nj = fp // tn
    return pl.pallas_call(
        _up_kernel,
        out_shape=jax.ShapeDtypeStruct((m, fp), BF16),
        grid=(m // tm, nj),
        in_specs=[pl.BlockSpec((tm, d), lambda i, j: (i, 0)),
                  pl.BlockSpec((None, None, d, tn), lambda i, j: (l, s, 0, j)),
                  pl.BlockSpec((None, None, d, tn), lambda i, j: (l, s, 0, j + nj))],
        out_specs=pl.BlockSpec((tm, tn), lambda i, j: (i, j)),
        compiler_params=_cparams(("arbitrary", "arbitrary")),
        name="ffn_up",
    )(h, w_up, w_up)


def _mm_ln_kernel(*refs, nk, sub, coef, alpha, has_next, rows):
    if has_next:
        a_ref, w_ref, x_ref, mc_ref, mn_ref, g_ref, b_ref, xo_ref, ho_ref = refs
    else:
        a_ref, w_ref, x_ref, mc_ref, g_ref, b_ref, xo_ref = refs
    k = pl.program_id(1)
    d = xo_ref.shape[1]
    slab = _pick(d, 512)

    def partial_products(first):
        a = a_ref[...]
        for n in range(d // slab):
            cols = slice(n * slab, (n + 1) * slab)
            p = jnp.dot(a, w_ref[:, cols], preferred_element_type=F32)
            if first:
                xo_ref[:, cols] = p
            else:
                xo_ref[:, cols] += p

    @pl.when(k == 0)
    def _():
        partial_products(True)

    @pl.when(k > 0)
    def _():
        partial_products(False)

    @pl.when(k == nk - 1)
    def _():
        gate = mc_ref[3 * sub + 2:3 * sub + 3, :] * coef
        ln_g = g_ref[...]
        ln_b = b_ref[...]
        if has_next:
            nsub = (sub + 1) % N_SUB
            shift_n = mn_ref[3 * nsub:3 * nsub + 1, :]
            scale_n = 1.0 + mn_ref[3 * nsub + 1:3 * nsub + 2, :]

        def body(r, carry):
            sl = pl.ds(pl.multiple_of(r * rows, rows), rows)
            y = alpha * x_ref[sl, :] + gate * xo_ref[sl, :]
            mu = jnp.mean(y, axis=-1, keepdims=True)
            yc = y - mu
            var = jnp.mean(yc * yc, axis=-1, keepdims=True)
            xn = yc * lax.rsqrt(var + LN_EPS) * ln_g + ln_b
            xo_ref[sl, :] = xn
            if has_next:
                ho_ref[sl, :] = (xn * scale_n + shift_n).astype(BF16)
            return carry

        lax.fori_loop(0, xo_ref.shape[0] // rows, body, 0)


def _mm_ln(a, w, w_idx, x, mod_cur, mod_next, ln_g, ln_b, row_of_tile, *, sub, coef, alpha):
    m, kdim = a.shape
    d = x.shape[1]
    tm = _pick(m, 512)
    tk = _pick(kdim, 1024)
    nk = kdim // tk
    has_next = mod_next is not None
    nlead = len(w_idx)
    w_spec = pl.BlockSpec((None,) * nlead + (tk, d), lambda i, k: tuple(w_idx) + (k, 0))
    mod_spec = pl.BlockSpec((None, N_SUB * 3, d), lambda i, k: (row_of_tile(i, tm), 0, 0))
    vec_spec = pl.BlockSpec((1, d), lambda i, k: (0, 0))
    tile_spec = pl.BlockSpec((tm, d), lambda i, k: (i, 0))
    x_spec = pl.BlockSpec((tm, d), lambda i, k: (i, 0), pipeline_mode=pl.Buffered(1))
    in_specs = [pl.BlockSpec((tm, tk), lambda i, k: (i, k)), w_spec, x_spec, mod_spec]
    args = [a, w, x, mod_cur]
    if has_next:
        in_specs.append(mod_spec)
        args.append(mod_next)
    in_specs += [vec_spec, vec_spec]
    args += [ln_g.reshape(1, d), ln_b.reshape(1, d)]
    out_shape = [jax.ShapeDtypeStruct((m, d), F32)]
    out_specs = [tile_spec]
    if has_next:
        out_shape.append(jax.ShapeDtypeStruct((m, d), BF16))
        out_specs.append(tile_spec)
    res = pl.pallas_call(
        functools.partial(_mm_ln_kernel, nk=nk, sub=sub, coef=coef, alpha=alpha, has_next=has_next,
                          rows=min(tm, LN_ROWS)),
        out_shape=out_shape,
        grid=(m // tm, nk),
        in_specs=in_specs,
        out_specs=out_specs,
        compiler_params=_cparams(("arbitrary", "arbitrary")),
        name="mm_ln",
    )(*args)
    return (res[0], res[1]) if has_next else (res[0], None)


def _proj_kernel(*refs, mode, scale, n_scaled):
    if mode == "rope":
        h_ref, w_ref, c_ref, sl_ref, sr_ref, o_ref = refs
    else:
        h_ref, w_ref = refs[:2]
        o_ref = refs[-1]
    acc = jnp.dot(h_ref[...], w_ref[...], preferred_element_type=F32)
    if mode == "f32":
        o_ref[...] = acc
    elif mode == "cache":
        o_ref[...] = acc.reshape(o_ref.shape)
    elif mode == "bf16":
        o_ref[...] = (acc * scale).astype(BF16)
    else:
        sc = jnp.where(pl.program_id(1) < n_scaled, scale, 1.0)
        cos = c_ref[...]
        s_lo = sl_ref[...]
        s_hi = sr_ref[...]
        for t in range(acc.shape[1] // LANES):
            x = acc[:, t * LANES:(t + 1) * LANES]
            r = x * cos + pltpu.roll(x, LANES - 16, 1) * s_lo + pltpu.roll(x, 16, 1) * s_hi
            o_ref[:, t * LANES:(t + 1) * LANES] = (r * sc).astype(BF16)


def _proj(h, w_in, l, *, row0, rows, col0, cols, mode, scale=1.0, n_scaled_cols=0, tables=None, seq=None,
          cache=None, depth=None):
    d = h.shape[1]
    tm = _pick(math.gcd(rows, row0), 1024)
    tn = _pick(math.gcd(math.gcd(cols, col0), n_scaled_cols), 512)
    i0, j0 = row0 // tm, col0 // tn
    in_specs = [pl.BlockSpec((tm, d), lambda i, j: (i + i0, 0)),
                pl.BlockSpec((None, d, tn), lambda i, j: (l, 0, j + j0))]
    args = [h, w_in]
    if mode == "cache":
        assert tm % seq == 0
        aliases = {}
        if cache is not None:
            in_specs.append(pl.BlockSpec(memory_space=pl.ANY))
            args.append(cache)
            aliases = {2: 0}
        return pl.pallas_call(
            functools.partial(_proj_kernel, mode=mode, scale=scale, n_scaled=0),
            out_shape=jax.ShapeDtypeStruct((rows // seq, depth, seq, cols), F32),
            grid=(rows // tm, cols // tn),
            in_specs=in_specs,
            out_specs=pl.BlockSpec((tm // seq, None, seq, tn), lambda i, j: (i, l, 0, j)),
            input_output_aliases=aliases,
            compiler_params=_cparams(("arbitrary", "arbitrary")),
            name="proj_cache",
        )(*args)
    if mode == "rope":
        tm = _pick(math.gcd(seq, tm), tm)
        i0 = row0 // tm
        in_specs[0] = pl.BlockSpec((tm, d), lambda i, j: (i + i0, 0))
        nt = seq // tm
        tab_spec = pl.BlockSpec((tm, LANES), lambda i, j: (i % nt, 0))
        in_specs += [tab_spec] * 3
        args += list(tables)
    out_dtype = F32 if mode == "f32" else BF16
    return pl.pallas_call(
        functools.partial(_proj_kernel, mode=mode, scale=scale, n_scaled=n_scaled_cols // tn),
        out_shape=jax.ShapeDtypeStruct((rows, cols), out_dtype),
        grid=(rows // tm, cols // tn),
        in_specs=in_specs,
        out_specs=pl.BlockSpec((tm, tn), lambda i, j: (i, j)),
        compiler_params=_cparams(("arbitrary", "arbitrary")),
        name="proj_" + mode,
    )(*args)


def _rope_tables(seq, dk):
    n_freq = dk // 4
    pos = jnp.arange(seq, dtype=jnp.int32)
    row = (pos // GRID_W).astype(F32)
    col = (pos % GRID_W).astype(F32)
    inv_freq = ROPE_THETA ** (-jnp.arange(n_freq, dtype=F32) / n_freq)
    ang_r = row[:, None] * inv_freq
    ang_c = col[:, None] * inv_freq
    zeros = jnp.zeros_like(ang_r)

    def lanes(a_r, a_c):
        one_map = jnp.concatenate([a_r[0], a_r[1], a_c[0], a_c[1]], axis=-1)
        return jnp.concatenate([one_map] * (LANES // dk), axis=-1)

    cos = lanes((jnp.cos(ang_r),) * 2, (jnp.cos(ang_c),) * 2)
    s_lo = lanes((-jnp.sin(ang_r), zeros), (-jnp.sin(ang_c), zeros))
    s_hi = lanes((zeros, jnp.sin(ang_r)), (zeros, jnp.sin(ang_c)))
    return cos, s_lo, s_hi


def _lambda(lam_ref, lam_init):
    lq1 = lam_ref[0:1, :]
    lk1 = lam_ref[1:2, :]
    lq2 = lam_ref[2:3, :]
    lk2 = lam_ref[3:4, :]
    return (jnp.exp(jnp.sum(lq1 * lk1, axis=-1, keepdims=True))
            - jnp.exp(jnp.sum(lq2 * lk2, axis=-1, keepdims=True)) + lam_init)


_NT = (((1,), (1,)), ((), ()))


def _head_attention(q, spans, keys, vals, s_scr, lam, gain, lam_init, dk):
    lq = q.shape[0]
    lane = lax.broadcasted_iota(jnp.int32, q.shape, 1)
    qf = q.astype(F32)
    qs = jnp.concatenate([jnp.where(lane < dk, qf, 0.0), jnp.where(lane >= dk, qf, 0.0)], axis=0).astype(BF16)
    mx = None
    for i, (off, n) in enumerate(spans):
        s = lax.dot_general(qs, keys(i), _NT, preferred_element_type=F32)
        s_scr[:, off:off + n] = s
        cm = jnp.max(s, axis=-1, keepdims=True)
        mx = cm if mx is None else jnp.maximum(mx, cm)
    den = None
    for off, n in spans:
        e = jnp.exp(s_scr[:, off:off + n] - mx)
        s_scr[:, off:off + n] = e
        cs = jnp.sum(e, axis=-1, keepdims=True)
        den = cs if den is None else den + cs
    inv = 1.0 / den
    c0 = inv[:lq]
    c1 = lam * inv[lq:]
    o = None
    for i, (off, n) in enumerate(spans):
        a = (s_scr[0:lq, off:off + n] * c0 - s_scr[lq:2 * lq, off:off + n] * c1).astype(BF16)
        t = jnp.dot(a, vals(i), preferred_element_type=F32)
        o = t if o is None else o + t
    ms = jnp.mean(o * o, axis=-1, keepdims=True)
    return o * lax.rsqrt(ms + RMS_EPS) * gain * (1.0 - lam_init)


def _attn_ctx_kernel(q_ref, k_ref, v_ref, lam_ref, g_ref, o_ref, s_scr, *, hb, lam_init, dk):
    lam = _lambda(lam_ref, lam_init)
    gain = g_ref[...]
    seq = k_ref.shape[0]
    for hh in range(hb):
        sl = slice(hh * LANES, (hh + 1) * LANES)
        y = _head_attention(q_ref[:, sl], [(0, seq)], lambda i: k_ref[:, sl].astype(BF16),
                            lambda i: v_ref[:, sl].astype(BF16), s_scr, lam, gain, lam_init, dk)
        o_ref[:, sl] = y.astype(BF16)


def _attn_lat_kernel(q_ref, kl_ref, vl_ref, kc_ref, vc_ref, lam_ref, g_ref, mix_ref, o_ref, *s_scr,
                     lam_init, dk, kchunk):
    del mix_ref
    lam = _lambda(lam_ref, lam_init)
    past = kc_ref.shape[0]
    seq = kl_ref.shape[0]
    spans = [(0, past)] + [(past + j * kchunk, kchunk) for j in range(seq // kchunk)]

    def keys(i):
        return kc_ref[...].astype(BF16) if i == 0 else kl_ref[(i - 1) * kchunk:i * kchunk, :]

    def vals(i):
        return vc_ref[...].astype(BF16) if i == 0 else vl_ref[(i - 1) * kchunk:i * kchunk, :]

    sub = q_ref.shape[0] // len(s_scr)
    for g, scr in enumerate(s_scr):
        rows = slice(g * sub, (g + 1) * sub)
        y = _head_attention(q_ref[rows, :], spans, keys, vals, scr, lam, g_ref[...], lam_init, dk)
        o_ref[rows, :] = y.astype(BF16)


def _attention_ctx(q, k_new, v_new, l, lam_p, gain, *, rows_total, mix_w, batch, seq, heads, lam_init, dk):
    hb = min(heads, 4)
    w = hb * LANES
    nh = heads // hb
    kv_spec = pl.BlockSpec((None, None, seq, w), lambda b, h: (b, l, 0, h))
    return pl.pallas_call(
        functools.partial(_attn_ctx_kernel, hb=hb, lam_init=lam_init, dk=dk),
        out_shape=jax.ShapeDtypeStruct((rows_total, mix_w), BF16),
        grid=(batch, nh),
        in_specs=[pl.BlockSpec((seq, w), lambda b, h: (b, h)), kv_spec, kv_spec,
                  pl.BlockSpec(lam_p.shape, lambda b, h: (0, 0)),
                  pl.BlockSpec((1, LANES), lambda b, h: (0, 0))],
        out_specs=pl.BlockSpec((seq, w), lambda b, h: (b, h)),
        scratch_shapes=[pltpu.VMEM((2 * seq, seq), F32)],
        compiler_params=_cparams(("arbitrary", "arbitrary")),
        name="attn_ctx",
    )(q, k_new, v_new, lam_p, gain)


def _attention_lat(qk, v, cache_k, cache_v, l, lam_p, gain, mix, *, row0, batch, seq, heads, lam_init, dk):
    w = LANES
    tq = _pick(seq, 512)
    nq = seq // tq
    i0 = row0 // tq
    kchunk = _pick(seq, 1024)
    n_sub = 2 if tq % (2 * 2 * SUBLANES) == 0 else 1
    past = cache_k.shape[2]
    cache_spec = pl.BlockSpec((None, None, past, w), lambda b, h, i: (b, l, 0, h))
    return pl.pallas_call(
        functools.partial(_attn_lat_kernel, lam_init=lam_init, dk=dk, kchunk=kchunk),
        out_shape=jax.ShapeDtypeStruct(mix.shape, mix.dtype),
        grid=(batch, heads, nq),
        in_specs=[pl.BlockSpec((tq, w), lambda b, h, i: (b * nq + i, h)),
                  pl.BlockSpec((seq, w), lambda b, h, i: (b, heads + h)),
                  pl.BlockSpec((seq, w), lambda b, h, i: (b, h)),
                  cache_spec, cache_spec,
                  pl.BlockSpec(lam_p.shape, lambda b, h, i: (0, 0)),
                  pl.BlockSpec((1, LANES), lambda b, h, i: (0, 0)),
                  pl.BlockSpec(memory_space=pl.ANY)],
        out_specs=pl.BlockSpec((tq, w), lambda b, h, i: (i0 + b * nq + i, h)),
        scratch_shapes=[pltpu.VMEM((2 * tq // n_sub, past + seq), F32)] * n_sub,
        input_output_aliases={7: 0},
        compiler_params=_cparams(("arbitrary", "arbitrary", "arbitrary")),
        name="attn_lat",
    )(qk, qk, v, cache_k, cache_v, lam_p, gain, mix)


def _conv_kernel(xc_ref, xp_ref, xn_ref, w_ref, b_ref, o_ref, buf, *, conv_w, n_ctx_chunks, cpc, cpl):
    i = pl.program_id(0)
    in_ctx = i < n_ctx_chunks
    pos = jnp.where(in_ctx, i % cpc, (i - n_ctx_chunks) % cpl)
    last = jnp.where(in_ctx, cpc - 1, cpl - 1)
    prev = xp_ref[...]
    nxt = xn_ref[...]
    buf[0:HALO, :] = jnp.where(pos == 0, jnp.zeros_like(prev), prev)
    buf[HALO:HALO + CHUNK, :] = xc_ref[...]
    buf[HALO + CHUNK:, :] = jnp.where(pos == last, jnp.zeros_like(nxt), nxt)
    pad = (conv_w - 1) // 2
    acc = None
    for k in range(conv_w):
        term = buf[HALO - pad + k:HALO - pad + k + CHUNK, :] * w_ref[k:k + 1, :]
        acc = term if acc is None else acc + term
    acc = acc + b_ref[...]
    o_ref[...] = acc * _sigmoid(acc)


def _conv_silu(zx, conv_w_l, conv_b_l, *, conv_ch, n_ctx, seq_ctx, seq_lat):
    m = zx.shape[0]
    kw = conv_w_l.shape[0]
    nb = m // HALO
    per = CHUNK // HALO
    return pl.pallas_call(
        functools.partial(_conv_kernel, conv_w=kw, n_ctx_chunks=n_ctx // CHUNK,
                          cpc=seq_ctx // CHUNK, cpl=seq_lat // CHUNK),
        out_shape=jax.ShapeDtypeStruct((m, conv_ch), F32),
        grid=(m // CHUNK,),
        in_specs=[pl.BlockSpec((CHUNK, conv_ch), lambda i: (i, 0)),
                  pl.BlockSpec((HALO, conv_ch), lambda i: (jnp.maximum(i * per - 1, 0), 0)),
                  pl.BlockSpec((HALO, conv_ch), lambda i: (jnp.minimum((i + 1) * per, nb - 1), 0)),
                  pl.BlockSpec((kw, conv_ch), lambda i: (0, 0)),
                  pl.BlockSpec((1, conv_ch), lambda i: (0, 0))],
        out_specs=pl.BlockSpec((CHUNK, conv_ch), lambda i: (i, 0)),
        scratch_shapes=[pltpu.VMEM((CHUNK + 2 * HALO, conv_ch), F32)],
        compiler_params=_cparams(("arbitrary",)),
        name="conv_silu",
    )(zx, zx, zx, conv_w_l, conv_b_l.reshape(1, conv_ch))


def _split3(x):
    hi = x.astype(BF16)
    r1 = x - hi.astype(F32)
    mid = r1.astype(BF16)
    lo = (r1 - mid.astype(F32)).astype(BF16)
    return hi, mid, lo


def _tri_sum(tri, x):
    hi, mid, lo = _split3(x)
    return (jnp.dot(tri, hi, preferred_element_type=F32) + jnp.dot(tri, mid, preferred_element_type=F32)
            + jnp.dot(tri, lo, preferred_element_type=F32))


def _colb(x, j):
    return jnp.broadcast_to(x[:, j:j + 1], (x.shape[0], LANES))


def _scan_dir(xa_ref, dt_ref, bias, a_neg, h_ref, y_ref, *, backward, n_heads, hpg, pdim, ssm_w, bc_cols):
    q = CHUNK
    ri = lax.broadcasted_iota(jnp.int32, (q, q), 0)
    ci = lax.broadcasted_iota(jnp.int32, (q, q), 1)
    keep = (ci >= ri) if backward else (ri >= ci)
    tri = jnp.where(keep, 1.0, 0.0).astype(BF16)
    lane = lax.broadcasted_iota(jnp.int32, (q, LANES), 1)
    first_half = lane < pdim

    raw = dt_ref[...] + bias
    dt = jnp.maximum(raw, 0.0) + jnp.log(1.0 + jnp.exp(-jnp.abs(raw)))
    acum = _tri_sum(tri, dt * a_neg)
    edge = acum[0:1, :] if backward else acum[q - 1:q, :]
    wend = jnp.exp(edge - acum) * dt
    eac = jnp.exp(acum)
    cdec = jnp.exp(edge)
    acum_t = acum.T
    dt_t = dt.T

    lane0 = n_heads if backward else 0
    heads_per_tile = LANES // pdim
    gmat = {}
    for t in range(n_heads // heads_per_tile):
        xs = xa_ref[:, t * LANES:(t + 1) * LANES]
        xs_b = xs.astype(BF16)
        y_intra = []
        w_cols, e_cols, d_cols = [], [], []
        grp = (t * heads_per_tile) // hpg
        if grp not in gmat:
            bm = xa_ref[:, ssm_w + grp * LANES:ssm_w + (grp + 1) * LANES].astype(BF16)
            cm = xa_ref[:, ssm_w + bc_cols + grp * LANES:ssm_w + bc_cols + (grp + 1) * LANES].astype(BF16)
            gmat = {grp: (bm, cm, lax.dot_general(cm, bm, _NT, preferred_element_type=F32))}
        bm, cm, g = gmat[grp]
        for hh in range(heads_per_tile):
            j = lane0 + t * heads_per_tile + hh
            seg = _colb(acum, j) - acum_t[j:j + 1, :]
            decay = jnp.exp(jnp.where(keep, seg, -jnp.inf))
            scores = (g * decay * dt_t[j:j + 1, :]).astype(BF16)
            y_intra.append(jnp.dot(scores, xs_b, preferred_element_type=F32))
            w_cols.append(_colb(wend, j))
            e_cols.append(_colb(eac, j))
            d_cols.append(jnp.broadcast_to(cdec[:, j:j + 1], (pdim, LANES)))
        if heads_per_tile == 2:
            yi = jnp.where(first_half, y_intra[0], y_intra[1])
            wp = jnp.where(first_half, w_cols[0], w_cols[1])
            ep = jnp.where(first_half, e_cols[0], e_cols[1])
        else:
            yi, wp, ep = y_intra[0], w_cols[0], e_cols[0]
        dp = jnp.concatenate(d_cols, axis=0)
        rows = slice(t * LANES, (t + 1) * LANES)
        h_old = h_ref[rows, :]
        y_inter = lax.dot_general(cm, h_old.astype(BF16), _NT, preferred_element_type=F32) * ep
        y_ref[:, t * LANES:(t + 1) * LANES] = yi + y_inter
        s_chunk = lax.dot_general((xs * wp).astype(BF16), bm, (((0,), (0,)), ((), ())),
                                  preferred_element_type=F32)
        h_ref[rows, :] = dp * h_old + s_chunk


def _scan_kernel(*refs, zero_init, emit_state, n_alias, nc, **dims):
    xf_ref, xb_ref, dtf_ref, dtb_ref, bias_ref, alog_ref = refs[:6]
    h0_ref = None if zero_init else refs[6]
    n_in = 6 + (0 if zero_init else 1) + n_alias
    yf_ref, yb_ref = refs[n_in:n_in + 2]
    hout_ref = refs[n_in + 2] if emit_state else None
    hf, hb = refs[-2:]
    c = pl.program_id(1)

    @pl.when(c == 0)
    def _():
        if zero_init:
            hf[...] = jnp.zeros_like(hf)
            hb[...] = jnp.zeros_like(hb)
        else:
            hf[...] = h0_ref[0]
            hb[...] = h0_ref[1]

    bias = bias_ref[...]
    a_neg = -jnp.exp(alog_ref[...])
    _scan_dir(xf_ref, dtf_ref, bias, a_neg, hf, yf_ref, backward=False, **dims)
    _scan_dir(xb_ref, dtb_ref, bias, a_neg, hb, yb_ref, backward=True, **dims)

    if emit_state:
        @pl.when(c == nc - 1)
        def _():
            hout_ref[0] = hf[...]
            hout_ref[1] = hb[...]


def _ssd_scan(xact, dt, bias, alog, l, *, row0, batch, seq, depth, h0=None, y_prev=None, states_prev=None,
              emit_state, n_heads, hpg, pdim, nstate, ssm_w, bc_cols):
    nc = seq // CHUNK
    c0 = row0 // CHUNK
    rows_total, conv_ch = xact.shape
    hp = n_heads * pdim
    zero_init = h0 is None
    fwd = lambda b, c: (c0 + b * nc + c, 0)
    bwd = lambda b, c: (c0 + b * nc + (nc - 1 - c), 0)
    vec = pl.BlockSpec((1, LANES), lambda b, c: (0, 0))
    in_specs = [pl.BlockSpec((CHUNK, conv_ch), fwd), pl.BlockSpec((CHUNK, conv_ch), bwd),
                pl.BlockSpec((CHUNK, LANES), fwd), pl.BlockSpec((CHUNK, LANES), bwd), vec, vec]
    args = [xact, xact, dt, dt, bias, alog]
    if not zero_init:
        in_specs.append(pl.BlockSpec((None, None, 2, hp, nstate), lambda b, c: (b, l, 0, 0, 0)))
        args.append(h0)
    aliases = {}
    if y_prev is not None:
        for k, arr in enumerate(y_prev):
            aliases[len(args)] = k
            in_specs.append(pl.BlockSpec(memory_space=pl.ANY))
            args.append(arr)
    out_shape = [jax.ShapeDtypeStruct((rows_total, ssm_w), F32), jax.ShapeDtypeStruct((rows_total, ssm_w), F32)]
    out_specs = [pl.BlockSpec((CHUNK, ssm_w), fwd), pl.BlockSpec((CHUNK, ssm_w), bwd)]
    if emit_state:
        out_shape.append(jax.ShapeDtypeStruct((batch, depth, 2, hp, nstate), F32))
        out_specs.append(pl.BlockSpec((None, None, 2, hp, nstate), lambda b, c: (b, l, 0, 0, 0)))
        if states_prev is not None:
            aliases[len(args)] = 2
            in_specs.append(pl.BlockSpec(memory_space=pl.ANY))
            args.append(states_prev)
    dims = dict(n_heads=n_heads, hpg=hpg, pdim=pdim, ssm_w=ssm_w, bc_cols=bc_cols)
    return pl.pallas_call(
        functools.partial(_scan_kernel, zero_init=zero_init, emit_state=emit_state, n_alias=len(aliases), nc=nc,
                          **dims),
        out_shape=out_shape,
        grid=(batch, nc),
        in_specs=in_specs,
        out_specs=out_specs,
        scratch_shapes=[pltpu.VMEM((hp, nstate), F32), pltpu.VMEM((hp, nstate), F32)],
        input_output_aliases=aliases,
        compiler_params=_cparams(("arbitrary", "arbitrary")),
        name="ssd_scan",
    )(*args)


def _ssd_out_kernel(yf_ref, yb_ref, xs_ref, z_ref, d_ref, g_ref, mix_ref, o_ref):
    del mix_ref
    y = yf_ref[...] + yb_ref[...] + d_ref[...] * xs_ref[...]
    z = z_ref[...]
    t = y * (z * _sigmoid(z))
    ms = jnp.mean(t * t, axis=-1, keepdims=True)
    o_ref[...] = (t * lax.rsqrt(ms + RMS_EPS) * g_ref[...]).astype(BF16)


def _ssd_out(yf, yb, xact, zx, d_lanes, gain, mix, *, ssm_w, conv_ch, attn_w):
    m = yf.shape[0]
    tm = _pick(m, 256)
    assert conv_ch % ssm_w == 0 and attn_w % ssm_w == 0
    zblk = conv_ch // ssm_w
    oblk = attn_w // ssm_w
    tile = pl.BlockSpec((tm, ssm_w), lambda i: (i, 0))
    vec = pl.BlockSpec((1, ssm_w), lambda i: (0, 0))
    return pl.pallas_call(
        _ssd_out_kernel,
        out_shape=jax.ShapeDtypeStruct(mix.shape, mix.dtype),
        grid=(m // tm,),
        in_specs=[tile, tile, tile, pl.BlockSpec((tm, ssm_w), lambda i: (i, zblk)), vec, vec,
                  pl.BlockSpec(memory_space=pl.ANY)],
        out_specs=pl.BlockSpec((tm, ssm_w), lambda i: (i, oblk)),
        input_output_aliases={6: 0},
        compiler_params=_cparams(("arbitrary",)),
        name="ssd_out",
    )(yf, yb, xact, zx, d_lanes, gain, mix)


def kernel(x_prompt, x_sample, c, cache_k, cache_v, state_ssm, c_ctx, w_mod, b_mod, ln_g, ln_b, w_ffn_up, w_ffn_down, w_in, conv_w, conv_b, dt_bias, a_log, d_skip, ssm_norm_g, lambda_q1, lambda_k1, lambda_q2, lambda_k2, subln_g, w_out):
    batch, seq, d = x_prompt.shape
    dbatch, dseq, _ = x_sample.shape
    depth = w_in.shape[0]
    heads, dk = cache_k.shape[3], cache_k.shape[5]
    dv = cache_v.shape[4]
    assert dv == LANES and 2 * dk == LANES, "one attention head must fill one 128-lane tile"
    attn_w = heads * dv
    n_heads, pdim, nstate = state_ssm.shape[3], state_ssm.shape[4], state_ssm.shape[5]
    assert nstate == LANES and LANES % pdim == 0 and 2 * n_heads <= LANES
    ssm_w = n_heads * pdim
    conv_ch = conv_w.shape[2]
    bc_cols = (conv_ch - ssm_w) // 2
    hpg = n_heads // (bc_cols // nstate)
    assert hpg % (LANES // pdim) == 0
    dff = w_ffn_down.shape[2]
    fp = -(-dff // 512) * 512
    alpha = (2 * depth) ** 0.25
    n_ctx, n_lat = batch * seq, dbatch * dseq
    assert seq % CHUNK == 0 and dseq % CHUNK == 0 and dseq % GRID_W == 0
    dt_cols = 2 * n_heads

    def row_of_tile(i, tm):
        return jnp.where(i * tm < n_ctx, 0, 1 + (i * tm - n_ctx) // dseq)

    pad_f = fp - dff
    w_up = jnp.concatenate(
        [jnp.pad(w_ffn_up[..., :dff], ((0, 0),) * 3 + ((0, pad_f),)),
         jnp.pad(w_ffn_up[..., dff:], ((0, 0),) * 3 + ((0, pad_f),))], axis=-1).astype(BF16)
    w_down = jnp.pad(w_ffn_down, ((0, 0), (0, 0), (0, pad_f), (0, 0))).astype(BF16)
    o_z = 3 * attn_w
    o_x = o_z + ssm_w
    o_dt = o_x + conv_ch
    w_in_b = jnp.concatenate(
        [w_in[..., :o_z], w_in[..., o_x:o_dt], w_in[..., o_z:o_x],
         jnp.pad(w_in[..., o_dt:], ((0, 0), (0, 0), (0, LANES - dt_cols)))], axis=-1).astype(BF16)
    w_out_b = w_out.astype(BF16)
    col_x = 3 * attn_w
    col_dt = col_x + conv_ch + ssm_w

    cvec = jnp.concatenate([c_ctx[None, :], c, jnp.zeros((MOD_ROWS - 1 - dbatch, d), F32)], axis=0)
    mod = _modulation(cvec, w_mod, b_mod)

    x = jnp.concatenate([x_prompt.reshape(n_ctx, d), x_sample.reshape(n_lat, d)], axis=0)
    tables = _rope_tables(dseq, dk)
    ck = cache_k.reshape(dbatch, depth, cache_k.shape[2], attn_w)
    cv = cache_v.reshape(dbatch, depth, cache_v.shape[2], attn_w)
    h0_all = state_ssm.reshape(dbatch, depth, 2, ssm_w, nstate)
    scale = dk ** -0.5

    h = _prep(x, mod[0], row_of_tile, _pick(n_ctx, 512))
    k_new = v_new = states = None
    for l in range(depth):
        lam_init = 0.8 - 0.6 * math.exp(-0.3 * l)
        lam_p = jnp.stack([lambda_q1[l], lambda_k1[l], lambda_q2[l], lambda_k2[l]], axis=0)
        pad_l = ((0, 0), (0, LANES - dt_cols))
        bias_l = jnp.pad(dt_bias[l].reshape(1, dt_cols), pad_l)
        alog_l = jnp.pad(a_log[l].reshape(1, dt_cols), pad_l)
        d_lanes = jnp.repeat(d_skip[l], pdim).reshape(1, ssm_w)

        act = _ffn_up(h, w_up, l, 0, fp)
        x, h = _mm_ln(act, w_down, (l, 0), x, mod[l], mod[l], ln_g[l, 0], ln_b[l, 0], row_of_tile,
                      sub=0, coef=0.5, alpha=alpha)

        zx = _proj(h, w_in_b, l, row0=0, rows=n_ctx + n_lat, col0=col_x, cols=conv_ch + ssm_w, mode="f32")
        dt = _proj(h, w_in_b, l, row0=0, rows=n_ctx + n_lat, col0=col_dt, cols=LANES, mode="f32")
        q_ctx = _proj(h, w_in_b, l, row0=0, rows=n_ctx, col0=0, cols=attn_w, mode="bf16", scale=scale)
        k_new = _proj(h, w_in_b, l, row0=0, rows=n_ctx, col0=attn_w, cols=attn_w, mode="cache", seq=seq,
                      cache=k_new, depth=depth)
        v_new = _proj(h, w_in_b, l, row0=0, rows=n_ctx, col0=2 * attn_w, cols=attn_w, mode="cache", seq=seq,
                      cache=v_new, depth=depth)
        qk_lat = _proj(h, w_in_b, l, row0=n_ctx, rows=n_lat, col0=0, cols=2 * attn_w, mode="rope",
                       scale=scale, n_scaled_cols=attn_w, tables=tables, seq=dseq)
        v_lat = _proj(h, w_in_b, l, row0=n_ctx, rows=n_lat, col0=2 * attn_w, cols=attn_w, mode="bf16")

        gain = subln_g[l].reshape(1, dv)
        mix = _attention_ctx(q_ctx, k_new, v_new, l, lam_p, gain, rows_total=n_ctx + n_lat,
                             mix_w=attn_w + ssm_w, batch=batch, seq=seq, heads=heads, lam_init=lam_init, dk=dk)
        mix = _attention_lat(qk_lat, v_lat, ck, cv, l, lam_p, gain, mix, row0=n_ctx, batch=dbatch, seq=dseq,
                             heads=heads, lam_init=lam_init, dk=dk)

        xact = _conv_silu(zx, conv_w[l], conv_b[l], conv_ch=conv_ch, n_ctx=n_ctx, seq_ctx=seq, seq_lat=dseq)
        sdims = dict(depth=depth, n_heads=n_heads, hpg=hpg, pdim=pdim, nstate=nstate, ssm_w=ssm_w,
                     bc_cols=bc_cols)
        yf, yb, states = _ssd_scan(xact, dt, bias_l, alog_l, l, row0=0, batch=batch, seq=seq,
                                   states_prev=states, emit_state=True, **sdims)
        yf, yb = _ssd_scan(xact, dt, bias_l, alog_l, l, row0=n_ctx, batch=dbatch, seq=dseq, h0=h0_all,
                           y_prev=(yf, yb), emit_state=False, **sdims)
        mix = _ssd_out(yf, yb, xact, zx, d_lanes, ssm_norm_g[l].reshape(1, ssm_w), mix, ssm_w=ssm_w,
                       conv_ch=conv_ch, attn_w=attn_w)

        x, h = _mm_ln(mix, w_out_b, (l,), x, mod[l], mod[l], ln_g[l, 1], ln_b[l, 1], row_of_tile,
                      sub=1, coef=1.0, alpha=alpha)

        act = _ffn_up(h, w_up, l, 1, fp)
        mod_next = mod[l + 1] if l + 1 < depth else None
        x, h = _mm_ln(act, w_down, (l, 1), x, mod[l], mod_next, ln_g[l, 2], ln_b[l, 2], row_of_tile,
                      sub=2, coef=0.5, alpha=alpha)

    y_prompt = x[:n_ctx].reshape(batch, seq, d)
    y_sample = x[n_ctx:].reshape(dbatch, dseq, d)
    return (y_prompt, y_sample, k_new.reshape(batch, depth, seq, heads, 2, dk),
            v_new.reshape(batch, depth, seq, heads, dv), states.reshape(batch, depth, 2, n_heads, pdim, nstate))
```

```python
import functools
import math

import jax
import jax.numpy as jnp
from jax import lax
from jax.experimental import pallas as pl
from jax.experimental.pallas import tpu as pltpu

GRID_W = 64
CHUNK = 128
ROPE_THETA = 10000.0
LN_EPS = 1e-5
RMS_EPS = 1e-6
N_SUB = 3
MOD_ROWS = 8
LANES = 128
SUBLANES = 8
HALO = SUBLANES
LN_ROWS = 32
LN_BLOCK = 128
CONV_ROWS = 256
VMEM_LIMIT = 56 * 1024 * 1024

F32 = jnp.float32
BF16 = jnp.bfloat16


def _cparams(sem):
    return pltpu.CompilerParams(dimension_semantics=sem, vmem_limit_bytes=VMEM_LIMIT)


def _pick(dim, pref):
    best = None
    t = LANES
    while t <= min(dim, pref):
        if dim % t == 0:
            best = t
        t += LANES
    return best if best is not None else dim


def _sigmoid(x):
    return 1.0 / (1.0 + jnp.exp(-x))


def _mod_kernel(c_ref, w_ref, b_ref, o_ref):
    c = c_ref[...]
    a = (c * _sigmoid(c)).astype(BF16)
    o_ref[...] = jnp.dot(a, w_ref[...].astype(BF16), preferred_element_type=F32) + b_ref[...]


def _modulation(cvec, w_mod, b_mod):
    depth, d, n = w_mod.shape
    tn = _pick(n, 512)
    out = pl.pallas_call(
        _mod_kernel,
        out_shape=jax.ShapeDtypeStruct((depth, MOD_ROWS, n), F32),
        grid=(depth, n // tn),
        in_specs=[pl.BlockSpec((MOD_ROWS, d), lambda l, j: (0, 0)),
                  pl.BlockSpec((None, d, tn), lambda l, j: (l, 0, j)),
                  pl.BlockSpec((None, 1, tn), lambda l, j: (l, 0, j))],
        out_specs=pl.BlockSpec((None, MOD_ROWS, tn), lambda l, j: (l, 0, j)),
        compiler_params=_cparams(("arbitrary", "arbitrary")),
        name="modulation",
    )(cvec, w_mod, b_mod.reshape(depth, 1, n))
    return out.reshape(depth, MOD_ROWS, N_SUB * 3, d)


def _prep_kernel(x_ref, m_ref, h_ref):
    shift = m_ref[0:1, :]
    scale = m_ref[1:2, :]
    h_ref[...] = (x_ref[...] * (1.0 + scale) + shift).astype(BF16)


def _prep(x, mod_l, row_of_tile, tm):
    m, d = x.shape
    return pl.pallas_call(
        _prep_kernel,
        out_shape=jax.ShapeDtypeStruct((m, d), BF16),
        grid=(m // tm,),
        in_specs=[pl.BlockSpec((tm, d), lambda i: (i, 0)),
                  pl.BlockSpec((None, N_SUB * 3, d), lambda i: (row_of_tile(i, tm), 0, 0))],
        out_specs=pl.BlockSpec((tm, d), lambda i: (i, 0)),
        compiler_params=_cparams(("arbitrary",)),
        name="prep",
    )(x, mod_l)


def _up_kernel(h_ref, wg_ref, wu_ref, o_ref):
    h = h_ref[...]
    g = jnp.dot(h, wg_ref[...], preferred_element_type=F32)
    u = jnp.dot(h, wu_ref[...], preferred_element_type=F32)
    o_ref[...] = (g * _sigmoid(g) * u).astype(o_ref.dtype)


def _ffn_up(h, w_up, l, s, fp):
    m, d = h.shape
    tm = _pick(m, 1024)
    tn = _pick(fp, 512)
    nj = fp // tn
    return pl.pallas_call(
        _up_kernel,
        out_shape=jax.ShapeDtypeStruct((m, fp), BF16),
        grid=(m // tm, nj),
        in_specs=[pl.BlockSpec((tm, d), lambda i, j: (i, 0)),
                  pl.BlockSpec((None, None, d, tn), lambda i, j: (l, s, 0, j)),
                  pl.BlockSpec((None, None, d, tn), lambda i, j: (l, s, 0, j + nj))],
        out_specs=pl.BlockSpec((tm, tn), lambda i, j: (i, j)),
        compiler_params=_cparams(("arbitrary", "arbitrary")),
        name="ffn_up",
    )(h, w_up, w_up)


def _mm_ln_kernel(*refs, nk, sub, coef, alpha, has_next, rows, blk):
    if has_next:
        a_ref, w_ref, mc_ref, mn_ref, g_ref, b_ref, x_hbm, xo_hbm, ho_hbm, acc, xin, xst, hst, s_in, s_xo, s_ho = refs
    else:
        a_ref, w_ref, mc_ref, g_ref, b_ref, x_hbm, xo_hbm, acc, xin, xst, s_in, s_xo = refs
    i = pl.program_id(0)
    k = pl.program_id(1)
    tm, d = acc.shape
    slab = _pick(d, 512)
    nblk = tm // blk
    row0 = i * tm

    def x_in(r, slot):
        return pltpu.make_async_copy(x_hbm.at[pl.ds(row0 + r * blk, blk), :], xin.at[slot], s_in.at[slot])

    def x_out(r, slot):
        return pltpu.make_async_copy(xst.at[slot], xo_hbm.at[pl.ds(row0 + r * blk, blk), :], s_xo.at[slot])

    def h_out(r, slot):
        return pltpu.make_async_copy(hst.at[slot], ho_hbm.at[pl.ds(row0 + r * blk, blk), :], s_ho.at[slot])

    @pl.when(k == nk - 1)
    def _():
        x_in(0, 0).start()

    def partial_products(first):
        a = a_ref[...]
        for n in range(d // slab):
            cols = slice(n * slab, (n + 1) * slab)
            p = jnp.dot(a, w_ref[:, cols], preferred_element_type=F32)
            if first:
                acc[:, cols] = p
            else:
                acc[:, cols] += p

    @pl.when(k == 0)
    def _():
        partial_products(True)

    @pl.when(k > 0)
    def _():
        partial_products(False)

    @pl.when(k == nk - 1)
    def _():
        gate = mc_ref[3 * sub + 2:3 * sub + 3, :] * coef
        ln_g = g_ref[...]
        ln_b = b_ref[...]
        if has_next:
            nsub = (sub + 1) % N_SUB
            shift_n = mn_ref[3 * nsub:3 * nsub + 1, :]
            scale_n = 1.0 + mn_ref[3 * nsub + 1:3 * nsub + 2, :]

        for r in range(nblk):
            slot = r % 2
            x_in(r, slot).wait()
            if r + 1 < nblk:
                x_in(r + 1, 1 - slot).start()
            if r >= 2:
                x_out(r - 2, slot).wait()
                if has_next:
                    h_out(r - 2, slot).wait()

            def body(j, carry, r=r, slot=slot):
                sl = pl.ds(pl.multiple_of(j * rows, rows), rows)
                asl = pl.ds(pl.multiple_of(r * blk + j * rows, rows), rows)
                y = alpha * xin[slot, sl, :] + gate * acc[asl, :]
                mu = jnp.mean(y, axis=-1, keepdims=True)
                yc = y - mu
                var = jnp.mean(yc * yc, axis=-1, keepdims=True)
                xn = yc * lax.rsqrt(var + LN_EPS) * ln_g + ln_b
                xst[slot, sl, :] = xn
                if has_next:
                    hst[slot, sl, :] = (xn * scale_n + shift_n).astype(BF16)
                return carry

            lax.fori_loop(0, blk // rows, body, 0)
            x_out(r, slot).start()
            if has_next:
                h_out(r, slot).start()
        for r in range(max(nblk - 2, 0), nblk):
            x_out(r, r % 2).wait()
            if has_next:
                h_out(r, r % 2).wait()


def _mm_ln(a, w, w_idx, x, mod_cur, mod_next, ln_g, ln_b, row_of_tile, *, sub, coef, alpha, tile_rows):
    m, kdim = a.shape
    d = x.shape[1]
    tm = _pick(math.gcd(m, tile_rows), 1024)
    tk = _pick(kdim, 1024)
    nk = kdim // tk
    blk = min(tm, LN_BLOCK)
    has_next = mod_next is not None
    nlead = len(w_idx)
    w_spec = pl.BlockSpec((None,) * nlead + (tk, d), lambda i, k: tuple(w_idx) + (k, 0))
    mod_spec = pl.BlockSpec((None, N_SUB * 3, d), lambda i, k: (row_of_tile(i, tm), 0, 0))
    vec_spec = pl.BlockSpec((1, d), lambda i, k: (0, 0))
    hbm_spec = pl.BlockSpec(memory_space=pl.ANY)
    in_specs = [pl.BlockSpec((tm, tk), lambda i, k: (i, k)), w_spec, mod_spec]
    args = [a, w, mod_cur]
    if has_next:
        in_specs.append(mod_spec)
        args.append(mod_next)
    in_specs += [vec_spec, vec_spec, hbm_spec]
    args += [ln_g.reshape(1, d), ln_b.reshape(1, d), x]
    out_shape = [jax.ShapeDtypeStruct((m, d), F32)]
    scratch = [pltpu.VMEM((tm, d), F32), pltpu.VMEM((2, blk, d), F32), pltpu.VMEM((2, blk, d), F32)]
    sems = [pltpu.SemaphoreType.DMA((2,)), pltpu.SemaphoreType.DMA((2,))]
    if has_next:
        out_shape.append(jax.ShapeDtypeStruct((m, d), BF16))
        scratch.append(pltpu.VMEM((2, blk, d), BF16))
        sems.append(pltpu.SemaphoreType.DMA((2,)))
    res = pl.pallas_call(
        functools.partial(_mm_ln_kernel, nk=nk, sub=sub, coef=coef, alpha=alpha, has_next=has_next,
                          rows=min(blk, LN_ROWS), blk=blk),
        out_shape=out_shape,
        grid=(m // tm, nk),
        in_specs=in_specs,
        out_specs=[hbm_spec] * len(out_shape),
        scratch_shapes=scratch + sems,
        compiler_params=_cparams(("arbitrary", "arbitrary")),
        name="mm_ln",
    )(*args)
    return (res[0], res[1]) if has_next else (res[0], None)


def _proj_kernel(*refs, mode, scale, n_scaled):
    if mode == "rope":
        h_ref, w_ref, c_ref, sl_ref, sr_ref, o_ref = refs
    else:
        h_ref, w_ref = refs[:2]
        o_ref = refs[-1]
    acc = jnp.dot(h_ref[...], w_ref[...], preferred_element_type=F32)
    if mode == "f32":
        o_ref[...] = acc
    elif mode == "cache":
        o_ref[...] = acc.reshape(o_ref.shape)
    elif mode == "bf16":
        o_ref[...] = (acc * scale).astype(BF16)
    else:
        sc = jnp.where(pl.program_id(1) < n_scaled, scale, 1.0)
        cos = c_ref[...]
        s_lo = sl_ref[...]
        s_hi = sr_ref[...]
        for t in range(acc.shape[1] // LANES):
            x = acc[:, t * LANES:(t + 1) * LANES]
            r = x * cos + pltpu.roll(x, LANES - 16, 1) * s_lo + pltpu.roll(x, 16, 1) * s_hi
            o_ref[:, t * LANES:(t + 1) * LANES] = (r * sc).astype(BF16)


def _proj(h, w_in, l, *, row0, rows, col0, cols, mode, scale=1.0, n_scaled_cols=0, tables=None, seq=None,
          cache=None, depth=None):
    d = h.shape[1]
    tm = _pick(math.gcd(rows, row0), 1024)
    tn = _pick(math.gcd(math.gcd(cols, col0), n_scaled_cols), 512)
    i0, j0 = row0 // tm, col0 // tn
    in_specs = [pl.BlockSpec((tm, d), lambda i, j: (i + i0, 0)),
                pl.BlockSpec((None, d, tn), lambda i, j: (l, 0, j + j0))]
    args = [h, w_in]
    if mode == "cache":
        assert tm % seq == 0
        aliases = {}
        if cache is not None:
            in_specs.append(pl.BlockSpec(memory_space=pl.ANY))
            args.append(cache)
            aliases = {2: 0}
        return pl.pallas_call(
            functools.partial(_proj_kernel, mode=mode, scale=scale, n_scaled=0),
            out_shape=jax.ShapeDtypeStruct((rows // seq, depth, seq, cols), F32),
            grid=(rows // tm, cols // tn),
            in_specs=in_specs,
            out_specs=pl.BlockSpec((tm // seq, None, seq, tn), lambda i, j: (i, l, 0, j)),
            input_output_aliases=aliases,
            compiler_params=_cparams(("arbitrary", "arbitrary")),
            name="proj_cache",
        )(*args)
    if mode == "rope":
        tm = _pick(math.gcd(seq, tm), tm)
        i0 = row0 // tm
        in_specs[0] = pl.BlockSpec((tm, d), lambda i, j: (i + i0, 0))
        nt = seq // tm
        tab_spec = pl.BlockSpec((tm, LANES), lambda i, j: (i % nt, 0))
        in_specs += [tab_spec] * 3
        args += list(tables)
    out_dtype = F32 if mode == "f32" else BF16
    return pl.pallas_call(
        functools.partial(_proj_kernel, mode=mode, scale=scale, n_scaled=n_scaled_cols // tn),
        out_shape=jax.ShapeDtypeStruct((rows, cols), out_dtype),
        grid=(rows // tm, cols // tn),
        in_specs=in_specs,
        out_specs=pl.BlockSpec((tm, tn), lambda i, j: (i, j)),
        compiler_params=_cparams(("arbitrary", "arbitrary")),
        name="proj_" + mode,
    )(*args)


def _rope_tables(seq, dk):
    n_freq = dk // 4
    pos = jnp.arange(seq, dtype=jnp.int32)
    row = (pos // GRID_W).astype(F32)
    col = (pos % GRID_W).astype(F32)
    inv_freq = ROPE_THETA ** (-jnp.arange(n_freq, dtype=F32) / n_freq)
    ang_r = row[:, None] * inv_freq
    ang_c = col[:, None] * inv_freq
    zeros = jnp.zeros_like(ang_r)

    def lanes(a_r, a_c):
        one_map = jnp.concatenate([a_r[0], a_r[1], a_c[0], a_c[1]], axis=-1)
        return jnp.concatenate([one_map] * (LANES // dk), axis=-1)

    cos = lanes((jnp.cos(ang_r),) * 2, (jnp.cos(ang_c),) * 2)
    s_lo = lanes((-jnp.sin(ang_r), zeros), (-jnp.sin(ang_c), zeros))
    s_hi = lanes((zeros, jnp.sin(ang_r)), (zeros, jnp.sin(ang_c)))
    return cos, s_lo, s_hi


def _lambda(lam_ref, lam_init):
    lq1 = lam_ref[0:1, :]
    lk1 = lam_ref[1:2, :]
    lq2 = lam_ref[2:3, :]
    lk2 = lam_ref[3:4, :]
    return (jnp.exp(jnp.sum(lq1 * lk1, axis=-1, keepdims=True))
            - jnp.exp(jnp.sum(lq2 * lk2, axis=-1, keepdims=True)) + lam_init)


_NT = (((1,), (1,)), ((), ()))


def _head_attention(q, spans, keys, vals, s_scr, lam, gain, lam_init, dk):
    (out,) = _attention_blocks([q], spans, keys, vals, [s_scr], lam, gain, lam_init, dk)
    return out


def _attention_blocks(qs_in, spans, keys, vals, scratches, lam, gain, lam_init, dk):
    n_blk = len(qs_in)
    st = [dict() for _ in range(n_blk)]
    for g, q in enumerate(qs_in):
        lane = lax.broadcasted_iota(jnp.int32, q.shape, 1)
        qf = q.astype(F32)
        st[g]["qs"] = jnp.concatenate([jnp.where(lane < dk, qf, 0.0), jnp.where(lane >= dk, qf, 0.0)],
                                      axis=0).astype(BF16)
        st[g]["lq"] = q.shape[0]

    def step(g, p, i):
        s_scr, c = scratches[g], st[g]
        off, n = spans[i]
        if p == 0:
            s = lax.dot_general(c["qs"], keys(i), _NT, preferred_element_type=F32)
            s_scr[:, off:off + n] = s
            cm = jnp.max(s, axis=-1, keepdims=True)
            c["mx"] = cm if i == 0 else jnp.maximum(c["mx"], cm)
        elif p == 1:
            e = jnp.exp(s_scr[:, off:off + n] - c["mx"])
            s_scr[:, off:off + n] = e
            cs = jnp.sum(e, axis=-1, keepdims=True)
            c["den"] = cs if i == 0 else c["den"] + cs
        else:
            lq = c["lq"]
            a = (s_scr[0:lq, off:off + n] * c["c0"] - s_scr[lq:2 * lq, off:off + n] * c["c1"]).astype(BF16)
            t = jnp.dot(a, vals(i), preferred_element_type=F32)
            c["o"] = t if i == 0 else c["o"] + t

    def finish(g, p):
        c = st[g]
        if p == 1:
            inv = 1.0 / c["den"]
            c["c0"] = inv[:c["lq"]]
            c["c1"] = lam * inv[c["lq"]:]
        elif p == 2:
            o = c["o"]
            ms = jnp.mean(o * o, axis=-1, keepdims=True)
            c["out"] = o * lax.rsqrt(ms + RMS_EPS) * gain * (1.0 - lam_init)

    for t in range(n_blk + 2):
        active = [(g, t - g) for g in range(n_blk) if 0 <= t - g <= 2]
        for i in range(len(spans)):
            for g, p in active:
                step(g, p, i)
        for g, p in active:
            finish(g, p)
    return [c["out"] for c in st]


def _attn_ctx_kernel(q_ref, k_ref, v_ref, lam_ref, g_ref, o_ref, s_scr, *, hb, lam_init, dk):
    lam = _lambda(lam_ref, lam_init)
    gain = g_ref[...]
    seq = k_ref.shape[0]
    for hh in range(hb):
        sl = slice(hh * LANES, (hh + 1) * LANES)
        y = _head_attention(q_ref[:, sl], [(0, seq)], lambda i: k_ref[:, sl].astype(BF16),
                            lambda i: v_ref[:, sl].astype(BF16), s_scr, lam, gain, lam_init, dk)
        o_ref[:, sl] = y.astype(BF16)


def _attn_lat_kernel(q_ref, kl_ref, vl_ref, kc_ref, vc_ref, lam_ref, g_ref, mix_ref, o_ref, *s_scr,
                     lam_init, dk, kchunk):
    del mix_ref
    lam = _lambda(lam_ref, lam_init)
    past = kc_ref.shape[0]
    seq = kl_ref.shape[0]
    spans = [(0, past)] + [(past + j * kchunk, kchunk) for j in range(seq // kchunk)]

    def keys(i):
        return kc_ref[...].astype(BF16) if i == 0 else kl_ref[(i - 1) * kchunk:i * kchunk, :]

    def vals(i):
        return vc_ref[...].astype(BF16) if i == 0 else vl_ref[(i - 1) * kchunk:i * kchunk, :]

    sub = q_ref.shape[0] // len(s_scr)
    blocks = [slice(g * sub, (g + 1) * sub) for g in range(len(s_scr))]
    outs = _attention_blocks([q_ref[rows, :] for rows in blocks], spans, keys, vals, list(s_scr), lam,
                             g_ref[...], lam_init, dk)
    for rows, y in zip(blocks, outs):
        o_ref[rows, :] = y.astype(BF16)


def _attention_ctx(q, k_new, v_new, l, lam_p, gain, *, rows_total, mix_w, batch, seq, heads, lam_init, dk):
    hb = min(heads, 4)
    w = hb * LANES
    nh = heads // hb
    kv_spec = pl.BlockSpec((None, None, seq, w), lambda b, h: (b, l, 0, h))
    return pl.pallas_call(
        functools.partial(_attn_ctx_kernel, hb=hb, lam_init=lam_init, dk=dk),
        out_shape=jax.ShapeDtypeStruct((rows_total, mix_w), BF16),
        grid=(batch, nh),
        in_specs=[pl.BlockSpec((seq, w), lambda b, h: (b, h)), kv_spec, kv_spec,
                  pl.BlockSpec(lam_p.shape, lambda b, h: (0, 0)),
                  pl.BlockSpec((1, LANES), lambda b, h: (0, 0))],
        out_specs=pl.BlockSpec((seq, w), lambda b, h: (b, h)),
        scratch_shapes=[pltpu.VMEM((2 * seq, seq), F32)],
        compiler_params=_cparams(("arbitrary", "arbitrary")),
        name="attn_ctx",
    )(q, k_new, v_new, lam_p, gain)


def _attention_lat(qk, v, cache_k, cache_v, l, lam_p, gain, mix, *, row0, batch, seq, heads, lam_init, dk):
    w = LANES
    tq = _pick(seq, 512)
    nq = seq // tq
    i0 = row0 // tq
    past = cache_k.shape[2]
    kchunk = _pick(seq, 1024)
    n_sub = 2 if tq % (2 * 2 * SUBLANES) == 0 else 1
    cache_spec = pl.BlockSpec((None, None, past, w), lambda b, h, i: (b, l, 0, h))
    return pl.pallas_call(
        functools.partial(_attn_lat_kernel, lam_init=lam_init, dk=dk, kchunk=kchunk),
        out_shape=jax.ShapeDtypeStruct(mix.shape, mix.dtype),
        grid=(batch, heads, nq),
        in_specs=[pl.BlockSpec((tq, w), lambda b, h, i: (b * nq + i, h)),
                  pl.BlockSpec((seq, w), lambda b, h, i: (b, heads + h)),
                  pl.BlockSpec((seq, w), lambda b, h, i: (b, h)),
                  cache_spec, cache_spec,
                  pl.BlockSpec(lam_p.shape, lambda b, h, i: (0, 0)),
                  pl.BlockSpec((1, LANES), lambda b, h, i: (0, 0)),
                  pl.BlockSpec(memory_space=pl.ANY)],
        out_specs=pl.BlockSpec((tq, w), lambda b, h, i: (i0 + b * nq + i, h)),
        scratch_shapes=[pltpu.VMEM((2 * tq // n_sub, past + seq), F32)] * n_sub,
        input_output_aliases={7: 0},
        compiler_params=_cparams(("arbitrary", "arbitrary", "arbitrary")),
        name="attn_lat",
    )(qk, qk, v, cache_k, cache_v, lam_p, gain, mix)


def _conv_kernel(xc_ref, xp_ref, xn_ref, w_ref, b_ref, o_ref, buf, *, conv_w, n_ctx_chunks, cpc, cpl):
    i = pl.program_id(0)
    in_ctx = i < n_ctx_chunks
    pos = jnp.where(in_ctx, i % cpc, (i - n_ctx_chunks) % cpl)
    last = jnp.where(in_ctx, cpc - 1, cpl - 1)
    prev = xp_ref[...]
    nxt = xn_ref[...]
    cb = xc_ref.shape[0]
    buf[0:HALO, :] = jnp.where(pos == 0, jnp.zeros_like(prev), prev)
    buf[HALO:HALO + cb, :] = xc_ref[...]
    buf[HALO + cb:, :] = jnp.where(pos == last, jnp.zeros_like(nxt), nxt)
    pad = (conv_w - 1) // 2
    acc = None
    for k in range(conv_w):
        term = buf[HALO - pad + k:HALO - pad + k + cb, :] * w_ref[k:k + 1, :]
        acc = term if acc is None else acc + term
    acc = acc + b_ref[...]
    o_ref[...] = acc * _sigmoid(acc)


def _conv_silu(zx, conv_w_l, conv_b_l, *, conv_ch, n_ctx, seq_ctx, seq_lat):
    m = zx.shape[0]
    kw = conv_w_l.shape[0]
    nb = m // HALO
    cb = math.gcd(math.gcd(seq_ctx, seq_lat), CONV_ROWS)
    per = cb // HALO
    return pl.pallas_call(
        functools.partial(_conv_kernel, conv_w=kw, n_ctx_chunks=n_ctx // cb, cpc=seq_ctx // cb, cpl=seq_lat // cb),
        out_shape=jax.ShapeDtypeStruct((m, conv_ch), F32),
        grid=(m // cb,),
        in_specs=[pl.BlockSpec((cb, conv_ch), lambda i: (i, 0)),
                  pl.BlockSpec((HALO, conv_ch), lambda i: (jnp.maximum(i * per - 1, 0), 0)),
                  pl.BlockSpec((HALO, conv_ch), lambda i: (jnp.minimum((i + 1) * per, nb - 1), 0)),
                  pl.BlockSpec((kw, conv_ch), lambda i: (0, 0)),
                  pl.BlockSpec((1, conv_ch), lambda i: (0, 0))],
        out_specs=pl.BlockSpec((cb, conv_ch), lambda i: (i, 0)),
        scratch_shapes=[pltpu.VMEM((cb + 2 * HALO, conv_ch), F32)],
        compiler_params=_cparams(("arbitrary",)),
        name="conv_silu",
    )(zx, zx, zx, conv_w_l, conv_b_l.reshape(1, conv_ch))


def _split3(x):
    hi = x.astype(BF16)
    r1 = x - hi.astype(F32)
    mid = r1.astype(BF16)
    lo = (r1 - mid.astype(F32)).astype(BF16)
    return hi, mid, lo


def _tri_sum(tri, x):
    hi, mid, lo = _split3(x)
    return (jnp.dot(tri, hi, preferred_element_type=F32) + jnp.dot(tri, mid, preferred_element_type=F32)
            + jnp.dot(tri, lo, preferred_element_type=F32))


def _colb(x, j):
    return jnp.broadcast_to(x[:, j:j + 1], (x.shape[0], LANES))


def _scan_dir(xa_ref, dt_ref, bias, a_neg, h_ref, y_ref, *, backward, n_heads, hpg, pdim, ssm_w, bc_cols):
    q = CHUNK
    ri = lax.broadcasted_iota(jnp.int32, (q, q), 0)
    ci = lax.broadcasted_iota(jnp.int32, (q, q), 1)
    keep = (ci >= ri) if backward else (ri >= ci)
    tri = jnp.where(keep, 1.0, 0.0).astype(BF16)
    lane = lax.broadcasted_iota(jnp.int32, (q, LANES), 1)
    first_half = lane < pdim

    raw = dt_ref[...] + bias
    dt = jnp.maximum(raw, 0.0) + jnp.log(1.0 + jnp.exp(-jnp.abs(raw)))
    acum = _tri_sum(tri, dt * a_neg)
    edge = acum[0:1, :] if backward else acum[q - 1:q, :]
    wend = jnp.exp(edge - acum) * dt
    eac = jnp.exp(acum)
    cdec = jnp.exp(edge)
    acum_t = acum.T
    dt_t = dt.T

    lane0 = n_heads if backward else 0
    heads_per_tile = LANES // pdim
    gmat = {}
    for t in range(n_heads // heads_per_tile):
        xs = xa_ref[:, t * LANES:(t + 1) * LANES]
        xs_b = xs.astype(BF16)
        y_intra = []
        w_cols, e_cols, d_cols = [], [], []
        grp = (t * heads_per_tile) // hpg
        if grp not in gmat:
            bm = xa_ref[:, ssm_w + grp * LANES:ssm_w + (grp + 1) * LANES].astype(BF16)
            cm = xa_ref[:, ssm_w + bc_cols + grp * LANES:ssm_w + bc_cols + (grp + 1) * LANES].astype(BF16)
            gmat = {grp: (bm, cm, lax.dot_general(cm, bm, _NT, preferred_element_type=F32))}
        bm, cm, g = gmat[grp]
        for hh in range(heads_per_tile):
            j = lane0 + t * heads_per_tile + hh
            seg = _colb(acum, j) - acum_t[j:j + 1, :]
            decay = jnp.exp(jnp.where(keep, seg, -jnp.inf))
            scores = (g * decay * dt_t[j:j + 1, :]).astype(BF16)
            y_intra.append(jnp.dot(scores, xs_b, preferred_element_type=F32))
            w_cols.append(_colb(wend, j))
            e_cols.append(_colb(eac, j))
            d_cols.append(jnp.broadcast_to(cdec[:, j:j + 1], (pdim, LANES)))
        if heads_per_tile == 2:
            yi = jnp.where(first_half, y_intra[0], y_intra[1])
            wp = jnp.where(first_half, w_cols[0], w_cols[1])
            ep = jnp.where(first_half, e_cols[0], e_cols[1])
        else:
            yi, wp, ep = y_intra[0], w_cols[0], e_cols[0]
        dp = jnp.concatenate(d_cols, axis=0)
        rows = slice(t * LANES, (t + 1) * LANES)
        h_old = h_ref[rows, :]
        y_inter = lax.dot_general(cm, h_old.astype(BF16), _NT, preferred_element_type=F32) * ep
        y_ref[:, t * LANES:(t + 1) * LANES] = yi + y_inter
        s_chunk = lax.dot_general((xs * wp).astype(BF16), bm, (((0,), (0,)), ((), ())),
                                  preferred_element_type=F32)
        h_ref[rows, :] = dp * h_old + s_chunk


def _scan_kernel(*refs, zero_init, emit_state, n_alias, nc, **dims):
    xf_ref, xb_ref, dtf_ref, dtb_ref, bias_ref, alog_ref = refs[:6]
    h0_ref = None if zero_init else refs[6]
    n_in = 6 + (0 if zero_init else 1) + n_alias
    yf_ref, yb_ref = refs[n_in:n_in + 2]
    hout_ref = refs[n_in + 2] if emit_state else None
    hf, hb = refs[-2:]
    c = pl.program_id(1)

    @pl.when(c == 0)
    def _():
        if zero_init:
            hf[...] = jnp.zeros_like(hf)
            hb[...] = jnp.zeros_like(hb)
        else:
            hf[...] = h0_ref[0]
            hb[...] = h0_ref[1]

    bias = bias_ref[...]
    a_neg = -jnp.exp(alog_ref[...])
    _scan_dir(xf_ref, dtf_ref, bias, a_neg, hf, yf_ref, backward=False, **dims)
    _scan_dir(xb_ref, dtb_ref, bias, a_neg, hb, yb_ref, backward=True, **dims)

    if emit_state:
        @pl.when(c == nc - 1)
        def _():
            hout_ref[0] = hf[...]
            hout_ref[1] = hb[...]


def _ssd_scan(xact, dt, bias, alog, l, *, row0, batch, seq, depth, h0=None, y_prev=None, states_prev=None,
              emit_state, n_heads, hpg, pdim, nstate, ssm_w, bc_cols):
    nc = seq // CHUNK
    c0 = row0 // CHUNK
    rows_total, conv_ch = xact.shape
    hp = n_heads * pdim
    zero_init = h0 is None
    fwd = lambda b, c: (c0 + b * nc + c, 0)
    bwd = lambda b, c: (c0 + b * nc + (nc - 1 - c), 0)
    vec = pl.BlockSpec((1, LANES), lambda b, c: (0, 0))
    in_specs = [pl.BlockSpec((CHUNK, conv_ch), fwd), pl.BlockSpec((CHUNK, conv_ch), bwd),
                pl.BlockSpec((CHUNK, LANES), fwd), pl.BlockSpec((CHUNK, LANES), bwd), vec, vec]
    args = [xact, xact, dt, dt, bias, alog]
    if not zero_init:
        in_specs.append(pl.BlockSpec((None, None, 2, hp, nstate), lambda b, c: (b, l, 0, 0, 0)))
        args.append(h0)
    aliases = {}
    if y_prev is not None:
        for k, arr in enumerate(y_prev):
            aliases[len(args)] = k
            in_specs.append(pl.BlockSpec(memory_space=pl.ANY))
            args.append(arr)
    out_shape = [jax.ShapeDtypeStruct((rows_total, ssm_w), F32), jax.ShapeDtypeStruct((rows_total, ssm_w), F32)]
    out_specs = [pl.BlockSpec((CHUNK, ssm_w), fwd), pl.BlockSpec((CHUNK, ssm_w), bwd)]
    if emit_state:
        out_shape.append(jax.ShapeDtypeStruct((batch, depth, 2, hp, nstate), F32))
        out_specs.append(pl.BlockSpec((None, None, 2, hp, nstate), lambda b, c: (b, l, 0, 0, 0)))
        if states_prev is not None:
            aliases[len(args)] = 2
            in_specs.append(pl.BlockSpec(memory_space=pl.ANY))
            args.append(states_prev)
    dims = dict(n_heads=n_heads, hpg=hpg, pdim=pdim, ssm_w=ssm_w, bc_cols=bc_cols)
    return pl.pallas_call(
        functools.partial(_scan_kernel, zero_init=zero_init, emit_state=emit_state, n_alias=len(aliases), nc=nc,
                          **dims),
        out_shape=out_shape,
        grid=(batch, nc),
        in_specs=in_specs,
        out_specs=out_specs,
        scratch_shapes=[pltpu.VMEM((hp, nstate), F32), pltpu.VMEM((hp, nstate), F32)],
        input_output_aliases=aliases,
        compiler_params=_cparams(("arbitrary", "arbitrary")),
        name="ssd_scan",
    )(*args)


def _ssd_out_kernel(yf_ref, yb_ref, xs_ref, z_ref, d_ref, g_ref, mix_ref, o_ref):
    del mix_ref
    y = yf_ref[...] + yb_ref[...] + d_ref[...] * xs_ref[...]
    z = z_ref[...]
    t = y * (z * _sigmoid(z))
    ms = jnp.mean(t * t, axis=-1, keepdims=True)
    o_ref[...] = (t * lax.rsqrt(ms + RMS_EPS) * g_ref[...]).astype(BF16)


def _ssd_out(yf, yb, xact, zx, d_lanes, gain, mix, *, ssm_w, conv_ch, attn_w):
    m = yf.shape[0]
    tm = _pick(m, 256)
    assert conv_ch % ssm_w == 0 and attn_w % ssm_w == 0
    zblk = conv_ch // ssm_w
    oblk = attn_w // ssm_w
    tile = pl.BlockSpec((tm, ssm_w), lambda i: (i, 0))
    vec = pl.BlockSpec((1, ssm_w), lambda i: (0, 0))
    return pl.pallas_call(
        _ssd_out_kernel,
        out_shape=jax.ShapeDtypeStruct(mix.shape, mix.dtype),
        grid=(m // tm,),
        in_specs=[tile, tile, tile, pl.BlockSpec((tm, ssm_w), lambda i: (i, zblk)), vec, vec,
                  pl.BlockSpec(memory_space=pl.ANY)],
        out_specs=pl.BlockSpec((tm, ssm_w), lambda i: (i, oblk)),
        input_output_aliases={6: 0},
        compiler_params=_cparams(("arbitrary",)),
        name="ssd_out",
    )(yf, yb, xact, zx, d_lanes, gain, mix)


def kernel(x_prompt, x_sample, c, cache_k, cache_v, state_ssm, c_ctx, w_mod, b_mod, ln_g, ln_b, w_ffn_up, w_ffn_down, w_in, conv_w, conv_b, dt_bias, a_log, d_skip, ssm_norm_g, lambda_q1, lambda_k1, lambda_q2, lambda_k2, subln_g, w_out):
    batch, seq, d = x_prompt.shape
    dbatch, dseq, _ = x_sample.shape
    depth = w_in.shape[0]
    heads, dk = cache_k.shape[3], cache_k.shape[5]
    dv = cache_v.shape[4]
    assert dv == LANES and 2 * dk == LANES, "one attention head must fill one 128-lane tile"
    attn_w = heads * dv
    n_heads, pdim, nstate = state_ssm.shape[3], state_ssm.shape[4], state_ssm.shape[5]
    assert nstate == LANES and LANES % pdim == 0 and 2 * n_heads <= LANES
    ssm_w = n_heads * pdim
    conv_ch = conv_w.shape[2]
    bc_cols = (conv_ch - ssm_w) // 2
    hpg = n_heads // (bc_cols // nstate)
    assert hpg % (LANES // pdim) == 0
    dff = w_ffn_down.shape[2]
    fp = -(-dff // 512) * 512
    alpha = (2 * depth) ** 0.25
    n_ctx, n_lat = batch * seq, dbatch * dseq
    assert seq % CHUNK == 0 and dseq % CHUNK == 0 and dseq % GRID_W == 0
    dt_cols = 2 * n_heads

    group_rows = math.gcd(n_ctx, dseq)

    def row_of_tile(i, tm):
        return jnp.where(i * tm < n_ctx, 0, 1 + (i * tm - n_ctx) // dseq)

    pad_f = fp - dff
    w_up = jnp.concatenate(
        [jnp.pad(w_ffn_up[..., :dff], ((0, 0),) * 3 + ((0, pad_f),)),
         jnp.pad(w_ffn_up[..., dff:], ((0, 0),) * 3 + ((0, pad_f),))], axis=-1).astype(BF16)
    w_down = jnp.pad(w_ffn_down, ((0, 0), (0, 0), (0, pad_f), (0, 0))).astype(BF16)
    o_z = 3 * attn_w
    o_x = o_z + ssm_w
    o_dt = o_x + conv_ch
    w_in_b = jnp.concatenate(
        [w_in[..., :o_z], w_in[..., o_x:o_dt], w_in[..., o_z:o_x],
         jnp.pad(w_in[..., o_dt:], ((0, 0), (0, 0), (0, LANES - dt_cols)))], axis=-1).astype(BF16)
    w_out_b = w_out.astype(BF16)
    col_x = 3 * attn_w
    col_dt = col_x + conv_ch + ssm_w

    cvec = jnp.concatenate([c_ctx[None, :], c, jnp.zeros((MOD_ROWS - 1 - dbatch, d), F32)], axis=0)
    mod = _modulation(cvec, w_mod, b_mod)

    x = jnp.concatenate([x_prompt.reshape(n_ctx, d), x_sample.reshape(n_lat, d)], axis=0)
    tables = _rope_tables(dseq, dk)
    ck = cache_k.reshape(dbatch, depth, cache_k.shape[2], attn_w)
    cv = cache_v.reshape(dbatch, depth, cache_v.shape[2], attn_w)
    h0_all = state_ssm.reshape(dbatch, depth, 2, ssm_w, nstate)
    scale = dk ** -0.5

    h = _prep(x, mod[0], row_of_tile, _pick(n_ctx, 512))
    k_new = v_new = states = None
    for l in range(depth):
        lam_init = 0.8 - 0.6 * math.exp(-0.3 * l)
        lam_p = jnp.stack([lambda_q1[l], lambda_k1[l], lambda_q2[l], lambda_k2[l]], axis=0)
        pad_l = ((0, 0), (0, LANES - dt_cols))
        bias_l = jnp.pad(dt_bias[l].reshape(1, dt_cols), pad_l)
        alog_l = jnp.pad(a_log[l].reshape(1, dt_cols), pad_l)
        d_lanes = jnp.repeat(d_skip[l], pdim).reshape(1, ssm_w)

        act = _ffn_up(h, w_up, l, 0, fp)
        x, h = _mm_ln(act, w_down, (l, 0), x, mod[l], mod[l], ln_g[l, 0], ln_b[l, 0], row_of_tile,
                      sub=0, coef=0.5, alpha=alpha, tile_rows=group_rows)

        zx = _proj(h, w_in_b, l, row0=0, rows=n_ctx + n_lat, col0=col_x, cols=conv_ch + ssm_w, mode="f32")
        dt = _proj(h, w_in_b, l, row0=0, rows=n_ctx + n_lat, col0=col_dt, cols=LANES, mode="f32")
        q_ctx = _proj(h, w_in_b, l, row0=0, rows=n_ctx, col0=0, cols=attn_w, mode="bf16", scale=scale)
        k_new = _proj(h, w_in_b, l, row0=0, rows=n_ctx, col0=attn_w, cols=attn_w, mode="cache", seq=seq,
                      cache=k_new, depth=depth)
        v_new = _proj(h, w_in_b, l, row0=0, rows=n_ctx, col0=2 * attn_w, cols=attn_w, mode="cache", seq=seq,
                      cache=v_new, depth=depth)
        qk_lat = _proj(h, w_in_b, l, row0=n_ctx, rows=n_lat, col0=0, cols=2 * attn_w, mode="rope",
                       scale=scale, n_scaled_cols=attn_w, tables=tables, seq=dseq)
        v_lat = _proj(h, w_in_b, l, row0=n_ctx, rows=n_lat, col0=2 * attn_w, cols=attn_w, mode="bf16")

        gain = subln_g[l].reshape(1, dv)
        mix = _attention_ctx(q_ctx, k_new, v_new, l, lam_p, gain, rows_total=n_ctx + n_lat,
                             mix_w=attn_w + ssm_w, batch=batch, seq=seq, heads=heads, lam_init=lam_init, dk=dk)
        mix = _attention_lat(qk_lat, v_lat, ck, cv, l, lam_p, gain, mix, row0=n_ctx, batch=dbatch, seq=dseq,
                             heads=heads, lam_init=lam_init, dk=dk)

        xact = _conv_silu(zx, conv_w[l], conv_b[l], conv_ch=conv_ch, n_ctx=n_ctx, seq_ctx=seq, seq_lat=dseq)
        sdims = dict(depth=depth, n_heads=n_heads, hpg=hpg, pdim=pdim, nstate=nstate, ssm_w=ssm_w,
                     bc_cols=bc_cols)
        yf, yb, states = _ssd_scan(xact, dt, bias_l, alog_l, l, row0=0, batch=batch, seq=seq,
                                   states_prev=states, emit_state=True, **sdims)
        yf, yb = _ssd_scan(xact, dt, bias_l, alog_l, l, row0=n_ctx, batch=dbatch, seq=dseq, h0=h0_all,
                           y_prev=(yf, yb), emit_state=False, **sdims)
        mix = _ssd_out(yf, yb, xact, zx, d_lanes, ssm_norm_g[l].reshape(1, ssm_w), mix, ssm_w=ssm_w,
                       conv_ch=conv_ch, attn_w=attn_w)

        x, h = _mm_ln(mix, w_out_b, (l,), x, mod[l], mod[l], ln_g[l, 1], ln_b[l, 1], row_of_tile,
                      sub=1, coef=1.0, alpha=alpha, tile_rows=group_rows)

        act = _ffn_up(h, w_up, l, 1, fp)
        mod_next = mod[l + 1] if l + 1 < depth else None
        x, h = _mm_ln(act, w_down, (l, 1), x, mod[l], mod_next, ln_g[l, 2], ln_b[l, 2], row_of_tile,
                      sub=2, coef=0.5, alpha=alpha, tile_rows=group_rows)

    y_prompt = x[:n_ctx].reshape(batch, seq, d)
    y_sample = x[n_ctx:].reshape(dbatch, dseq, d)
    return (y_prompt, y_sample, k_new.reshape(batch, depth, seq, heads, 2, dk),
            v_new.reshape(batch, depth, seq, heads, dv), states.reshape(batch, depth, 2, n_heads, pdim, nstate))
```

```python
import functools
import math

import jax
import jax.numpy as jnp
from jax import lax
from jax.experimental import pallas as pl
from jax.experimental.pallas import tpu as pltpu

GRID_W = 64
CHUNK = 128
ROPE_THETA = 10000.0
LN_EPS = 1e-5
RMS_EPS = 1e-6
N_SUB = 3
MOD_ROWS = 8
LANES = 128
SUBLANES = 8
HALO = SUBLANES
LN_ROWS = 32
LN_BLOCK = 128
CONV_ROWS = 256
VMEM_LIMIT = 56 * 1024 * 1024

F32 = jnp.float32
BF16 = jnp.bfloat16


def _cparams(sem):
    return pltpu.CompilerParams(dimension_semantics=sem, vmem_limit_bytes=VMEM_LIMIT)


def _pick(dim, pref):
    best = None
    t = LANES
    while t <= min(dim, pref):
        if dim % t == 0:
            best = t
        t += LANES
    return best if best is not None else dim


def _sigmoid(x):
    return 1.0 / (1.0 + jnp.exp(-x))


def _mod_kernel(c_ref, w_ref, b_ref, o_ref):
    c = c_ref[...]
    a = (c * _sigmoid(c)).astype(BF16)
    o_ref[...] = jnp.dot(a, w_ref[...].astype(BF16), preferred_element_type=F32) + b_ref[...]


def _modulation(cvec, w_mod, b_mod):
    depth, d, n = w_mod.shape
    tn = _pick(n, 512)
    out = pl.pallas_call(
        _mod_kernel,
        out_shape=jax.ShapeDtypeStruct((depth, MOD_ROWS, n), F32),
        grid=(depth, n // tn),
        in_specs=[pl.BlockSpec((MOD_ROWS, d), lambda l, j: (0, 0)),
                  pl.BlockSpec((None, d, tn), lambda l, j: (l, 0, j)),
                  pl.BlockSpec((None, 1, tn), lambda l, j: (l, 0, j))],
        out_specs=pl.BlockSpec((None, MOD_ROWS, tn), lambda l, j: (l, 0, j)),
        compiler_params=_cparams(("arbitrary", "arbitrary")),
        name="modulation",
    )(cvec, w_mod, b_mod.reshape(depth, 1, n))
    return out.reshape(depth, MOD_ROWS, N_SUB * 3, d)


def _prep_kernel(x_ref, m_ref, h_ref):
    shift = m_ref[0:1, :]
    scale = m_ref[1:2, :]
    h_ref[...] = (x_ref[...] * (1.0 + scale) + shift).astype(BF16)


def _prep(x, mod_l, row_of_tile, tm):
    m, d = x.shape
    return pl.pallas_call(
        _prep_kernel,
        out_shape=jax.ShapeDtypeStruct((m, d), BF16),
        grid=(m // tm,),
        in_specs=[pl.BlockSpec((tm, d), lambda i: (i, 0)),
                  pl.BlockSpec((None, N_SUB * 3, d), lambda i: (row_of_tile(i, tm), 0, 0))],
        out_specs=pl.BlockSpec((tm, d), lambda i: (i, 0)),
        compiler_params=_cparams(("arbitrary",)),
        name="prep",
    )(x, mod_l)


def _up_kernel(h_ref, wg_ref, wu_ref, o_ref):
    h = h_ref[...]
    g = jnp.dot(h, wg_ref[...], preferred_element_type=F32)
    u = jnp.dot(h, wu_ref[...], preferred_element_type=F32)
    o_ref[...] = (g * _sigmoid(g) * u).astype(o_ref.dtype)


def _ffn_up(h, w_up, l, s, fp):
    m, d = h.shape
    tm = _pick(m, 1024)
    tn = _pick(fp, 512)
    nj = fp // tn
    return pl.pallas_call(
        _up_kernel,
        out_shape=jax.ShapeDtypeStruct((m, fp), BF16),
        grid=(m // tm, nj),
        in_specs=[pl.BlockSpec((tm, d), lambda i, j: (i, 0)),
                  pl.BlockSpec((None, None, d, tn), lambda i, j: (l, s, 0, j)),
                  pl.BlockSpec((None, None, d, tn), lambda i, j: (l, s, 0, j + nj))],
        out_specs=pl.BlockSpec((tm, tn), lambda i, j: (i, j)),
        compiler_params=_cparams(("arbitrary", "arbitrary")),
        name="ffn_up",
    )(h, w_up, w_up)


def _mm_ln_kernel(*refs, nk, sub, coef, alpha, has_next, rows, blk):
    if has_next:
        a_ref, w_ref, mc_ref, mn_ref, g_ref, b_ref, x_hbm, xo_hbm, ho_hbm, acc, xin, xst, hst, s_in, s_xo, s_ho = refs
    else:
        a_ref, w_ref, mc_ref, g_ref, b_ref, x_hbm, xo_hbm, acc, xin, xst, s_in, s_xo = refs
    i = pl.program_id(0)
    k = pl.program_id(1)
    tm, d = acc.shape
    slab = _pick(d, 512)
    nblk = tm // blk
    row0 = i * tm

    def x_in(r, slot):
        return pltpu.make_async_copy(x_hbm.at[pl.ds(row0 + r * blk, blk), :], xin.at[slot], s_in.at[slot])

    def x_out(r, slot):
        return pltpu.make_async_copy(xst.at[slot], xo_hbm.at[pl.ds(row0 + r * blk, blk), :], s_xo.at[slot])

    def h_out(r, slot):
        return pltpu.make_async_copy(hst.at[slot], ho_hbm.at[pl.ds(row0 + r * blk, blk), :], s_ho.at[slot])

    @pl.when(k == nk - 1)
    def _():
        x_in(0, 0).start()

    def partial_products(first):
        a = a_ref[...]
        for n in range(d // slab):
            cols = slice(n * slab, (n + 1) * slab)
            p = jnp.dot(a, w_ref[:, cols], preferred_element_type=F32)
            if first:
                acc[:, cols] = p
            else:
                acc[:, cols] += p

    @pl.when(k == 0)
    def _():
        partial_products(True)

    @pl.when(k > 0)
    def _():
        partial_products(False)

    @pl.when(k == nk - 1)
    def _():
        gate = mc_ref[3 * sub + 2:3 * sub + 3, :] * coef
        ln_g = g_ref[...]
        ln_b = b_ref[...]
        if has_next:
            nsub = (sub + 1) % N_SUB
            shift_n = mn_ref[3 * nsub:3 * nsub + 1, :]
            scale_n = 1.0 + mn_ref[3 * nsub + 1:3 * nsub + 2, :]

        for r in range(nblk):
            slot = r % 2
            x_in(r, slot).wait()
            if r + 1 < nblk:
                x_in(r + 1, 1 - slot).start()
            if r >= 2:
                x_out(r - 2, slot).wait()
                if has_next:
                    h_out(r - 2, slot).wait()

            def body(j, carry, r=r, slot=slot):
                sl = pl.ds(pl.multiple_of(j * rows, rows), rows)
                asl = pl.ds(pl.multiple_of(r * blk + j * rows, rows), rows)
                y = alpha * xin[slot, sl, :] + gate * acc[asl, :]
                mu = jnp.mean(y, axis=-1, keepdims=True)
                yc = y - mu
                var = jnp.mean(yc * yc, axis=-1, keepdims=True)
                xn = yc * lax.rsqrt(var + LN_EPS) * ln_g + ln_b
                xst[slot, sl, :] = xn
                if has_next:
                    hst[slot, sl, :] = (xn * scale_n + shift_n).astype(BF16)
                return carry

            lax.fori_loop(0, blk // rows, body, 0)
            x_out(r, slot).start()
            if has_next:
                h_out(r, slot).start()
        for r in range(max(nblk - 2, 0), nblk):
            x_out(r, r % 2).wait()
            if has_next:
                h_out(r, r % 2).wait()


def _mm_ln(a, w, w_idx, x, mod_cur, mod_next, ln_g, ln_b, row_of_tile, *, sub, coef, alpha, tile_rows):
    m, kdim = a.shape
    d = x.shape[1]
    tm = _pick(math.gcd(m, tile_rows), 1024)
    tk = _pick(kdim, 1024)
    nk = kdim // tk
    blk = min(tm, LN_BLOCK)
    has_next = mod_next is not None
    nlead = len(w_idx)
    w_spec = pl.BlockSpec((None,) * nlead + (tk, d), lambda i, k: tuple(w_idx) + (k, 0))
    mod_spec = pl.BlockSpec((None, N_SUB * 3, d), lambda i, k: (row_of_tile(i, tm), 0, 0))
    vec_spec = pl.BlockSpec((1, d), lambda i, k: (0, 0))
    hbm_spec = pl.BlockSpec(memory_space=pl.ANY)
    in_specs = [pl.BlockSpec((tm, tk), lambda i, k: (i, k)), w_spec, mod_spec]
    args = [a, w, mod_cur]
    if has_next:
        in_specs.append(mod_spec)
        args.append(mod_next)
    in_specs += [vec_spec, vec_spec, hbm_spec]
    args += [ln_g.reshape(1, d), ln_b.reshape(1, d), x]
    out_shape = [jax.ShapeDtypeStruct((m, d), F32)]
    scratch = [pltpu.VMEM((tm, d), F32), pltpu.VMEM((2, blk, d), F32), pltpu.VMEM((2, blk, d), F32)]
    sems = [pltpu.SemaphoreType.DMA((2,)), pltpu.SemaphoreType.DMA((2,))]
    if has_next:
        out_shape.append(jax.ShapeDtypeStruct((m, d), BF16))
        scratch.append(pltpu.VMEM((2, blk, d), BF16))
        sems.append(pltpu.SemaphoreType.DMA((2,)))
    res = pl.pallas_call(
        functools.partial(_mm_ln_kernel, nk=nk, sub=sub, coef=coef, alpha=alpha, has_next=has_next,
                          rows=min(blk, LN_ROWS), blk=blk),
        out_shape=out_shape,
        grid=(m // tm, nk),
        in_specs=in_specs,
        out_specs=[hbm_spec] * len(out_shape),
        scratch_shapes=scratch + sems,
        compiler_params=_cparams(("arbitrary", "arbitrary")),
        name="mm_ln",
    )(*args)
    return (res[0], res[1]) if has_next else (res[0], None)


def _proj_kernel(*refs, mode, scale, n_scaled):
    if mode == "rope":
        h_ref, w_ref, c_ref, sl_ref, sr_ref, o_ref = refs
    else:
        h_ref, w_ref = refs[:2]
        o_ref = refs[-1]
    acc = jnp.dot(h_ref[...], w_ref[...], preferred_element_type=F32)
    if mode == "f32":
        o_ref[...] = acc
    elif mode == "cache":
        o_ref[...] = acc.reshape(o_ref.shape)
    elif mode == "bf16":
        o_ref[...] = (acc * scale).astype(BF16)
    else:
        sc = jnp.where(pl.program_id(1) < n_scaled, scale, 1.0)
        cos = c_ref[...]
        s_lo = sl_ref[...]
        s_hi = sr_ref[...]
        for t in range(acc.shape[1] // LANES):
            x = acc[:, t * LANES:(t + 1) * LANES]
            r = x * cos + pltpu.roll(x, LANES - 16, 1) * s_lo + pltpu.roll(x, 16, 1) * s_hi
            o_ref[:, t * LANES:(t + 1) * LANES] = (r * sc).astype(BF16)


def _proj(h, w_in, l, *, row0, rows, col0, cols, mode, scale=1.0, n_scaled_cols=0, tables=None, seq=None,
          cache=None, depth=None):
    d = h.shape[1]
    tm = _pick(math.gcd(rows, row0), 1024)
    tn = _pick(math.gcd(math.gcd(cols, col0), n_scaled_cols), 512)
    i0, j0 = row0 // tm, col0 // tn
    in_specs = [pl.BlockSpec((tm, d), lambda i, j: (i + i0, 0)),
                pl.BlockSpec((None, d, tn), lambda i, j: (l, 0, j + j0))]
    args = [h, w_in]
    if mode == "cache":
        assert tm % seq == 0
        aliases = {}
        if cache is not None:
            in_specs.append(pl.BlockSpec(memory_space=pl.ANY))
            args.append(cache)
            aliases = {2: 0}
        return pl.pallas_call(
            functools.partial(_proj_kernel, mode=mode, scale=scale, n_scaled=0),
            out_shape=jax.ShapeDtypeStruct((rows // seq, depth, seq, cols), F32),
            grid=(rows // tm, cols // tn),
            in_specs=in_specs,
            out_specs=pl.BlockSpec((tm // seq, None, seq, tn), lambda i, j: (i, l, 0, j)),
            input_output_aliases=aliases,
            compiler_params=_cparams(("arbitrary", "arbitrary")),
            name="proj_cache",
        )(*args)
    if mode == "rope":
        tm = _pick(math.gcd(seq, tm), tm)
        i0 = row0 // tm
        in_specs[0] = pl.BlockSpec((tm, d), lambda i, j: (i + i0, 0))
        nt = seq // tm
        tab_spec = pl.BlockSpec((tm, LANES), lambda i, j: (i % nt, 0))
        in_specs += [tab_spec] * 3
        args += list(tables)
    out_dtype = F32 if mode == "f32" else BF16
    return pl.pallas_call(
        functools.partial(_proj_kernel, mode=mode, scale=scale, n_scaled=n_scaled_cols // tn),
        out_shape=jax.ShapeDtypeStruct((rows, cols), out_dtype),
        grid=(rows // tm, cols // tn),
        in_specs=in_specs,
        out_specs=pl.BlockSpec((tm, tn), lambda i, j: (i, j)),
        compiler_params=_cparams(("arbitrary", "arbitrary")),
        name="proj_" + mode,
    )(*args)


def _rope_tables(seq, dk):
    n_freq = dk // 4
    pos = jnp.arange(seq, dtype=jnp.int32)
    row = (pos // GRID_W).astype(F32)
    col = (pos % GRID_W).astype(F32)
    inv_freq = ROPE_THETA ** (-jnp.arange(n_freq, dtype=F32) / n_freq)
    ang_r = row[:, None] * inv_freq
    ang_c = col[:, None] * inv_freq
    zeros = jnp.zeros_like(ang_r)

    def lanes(a_r, a_c):
        one_map = jnp.concatenate([a_r[0], a_r[1], a_c[0], a_c[1]], axis=-1)
        return jnp.concatenate([one_map] * (LANES // dk), axis=-1)

    cos = lanes((jnp.cos(ang_r),) * 2, (jnp.cos(ang_c),) * 2)
    s_lo = lanes((-jnp.sin(ang_r), zeros), (-jnp.sin(ang_c), zeros))
    s_hi = lanes((zeros, jnp.sin(ang_r)), (zeros, jnp.sin(ang_c)))
    return cos, s_lo, s_hi


def _lambda(lam_ref, lam_init):
    lq1 = lam_ref[0:1, :]
    lk1 = lam_ref[1:2, :]
    lq2 = lam_ref[2:3, :]
    lk2 = lam_ref[3:4, :]
    return (jnp.exp(jnp.sum(lq1 * lk1, axis=-1, keepdims=True))
            - jnp.exp(jnp.sum(lq2 * lk2, axis=-1, keepdims=True)) + lam_init)


_NT = (((1,), (1,)), ((), ()))


def _head_attention(q, spans, keys, vals, s_scr, lam, gain, lam_init, dk):
    (out,) = _attention_blocks([q], spans, keys, vals, [s_scr], lam, gain, lam_init, dk)
    return out


def _attention_blocks(qs_in, spans, keys, vals, scratches, lam, gain, lam_init, dk):
    n_blk = len(qs_in)
    st = [dict() for _ in range(n_blk)]
    for g, q in enumerate(qs_in):
        lane = lax.broadcasted_iota(jnp.int32, q.shape, 1)
        qf = q.astype(F32)
        st[g]["qs"] = jnp.concatenate([jnp.where(lane < dk, qf, 0.0), jnp.where(lane >= dk, qf, 0.0)],
                                      axis=0).astype(BF16)
        st[g]["lq"] = q.shape[0]

    def step(g, p, i):
        s_scr, c = scratches[g], st[g]
        off, n = spans[i]
        if p == 0:
            s = lax.dot_general(c["qs"], keys(i), _NT, preferred_element_type=F32)
            s_scr[:, off:off + n] = s
            cm = jnp.max(s, axis=-1, keepdims=True)
            c["mx"] = cm if i == 0 else jnp.maximum(c["mx"], cm)
        elif p == 1:
            e = jnp.exp(s_scr[:, off:off + n] - c["mx"])
            s_scr[:, off:off + n] = e
            cs = jnp.sum(e, axis=-1, keepdims=True)
            c["den"] = cs if i == 0 else c["den"] + cs
        else:
            lq = c["lq"]
            a = (s_scr[0:lq, off:off + n] * c["c0"] - s_scr[lq:2 * lq, off:off + n] * c["c1"]).astype(BF16)
            t = jnp.dot(a, vals(i), preferred_element_type=F32)
            c["o"] = t if i == 0 else c["o"] + t

    def finish(g, p):
        c = st[g]
        if p == 1:
            inv = 1.0 / c["den"]
            c["c0"] = inv[:c["lq"]]
            c["c1"] = lam * inv[c["lq"]:]
        elif p == 2:
            o = c["o"]
            ms = jnp.mean(o * o, axis=-1, keepdims=True)
            c["out"] = o * lax.rsqrt(ms + RMS_EPS) * gain * (1.0 - lam_init)

    for t in range(n_blk + 2):
        active = [(g, t - g) for g in range(n_blk) if 0 <= t - g <= 2]
        for i in range(len(spans)):
            for g, p in active:
                step(g, p, i)
        for g, p in active:
            finish(g, p)
    return [c["out"] for c in st]


def _attn_ctx_kernel(q_ref, k_ref, v_ref, lam_ref, g_ref, o_ref, s_scr, *, hb, lam_init, dk):
    lam = _lambda(lam_ref, lam_init)
    gain = g_ref[...]
    seq = k_ref.shape[0]
    for hh in range(hb):
        sl = slice(hh * LANES, (hh + 1) * LANES)
        y = _head_attention(q_ref[:, sl], [(0, seq)], lambda i: k_ref[:, sl].astype(BF16),
                            lambda i: v_ref[:, sl].astype(BF16), s_scr, lam, gain, lam_init, dk)
        o_ref[:, sl] = y.astype(BF16)


def _attn_lat_kernel(q_ref, kl_ref, vl_ref, kc_ref, vc_ref, lam_ref, g_ref, mix_ref, o_ref, *s_scr,
                     lam_init, dk, kchunk):
    del mix_ref
    lam = _lambda(lam_ref, lam_init)
    past = kc_ref.shape[0]
    seq = kl_ref.shape[0]
    spans = [(0, past)] + [(past + j * kchunk, kchunk) for j in range(seq // kchunk)]

    def keys(i):
        return kc_ref[...].astype(BF16) if i == 0 else kl_ref[(i - 1) * kchunk:i * kchunk, :]

    def vals(i):
        return vc_ref[...].astype(BF16) if i == 0 else vl_ref[(i - 1) * kchunk:i * kchunk, :]

    sub = q_ref.shape[0] // len(s_scr)
    blocks = [slice(g * sub, (g + 1) * sub) for g in range(len(s_scr))]
    outs = _attention_blocks([q_ref[rows, :] for rows in blocks], spans, keys, vals, list(s_scr), lam,
                             g_ref[...], lam_init, dk)
    for rows, y in zip(blocks, outs):
        o_ref[rows, :] = y.astype(BF16)


def _attention_ctx(q, k_new, v_new, l, lam_p, gain, *, rows_total, mix_w, batch, seq, heads, lam_init, dk):
    hb = min(heads, 4)
    w = hb * LANES
    nh = heads // hb
    kv_spec = pl.BlockSpec((None, None, seq, w), lambda b, h: (b, l, 0, h))
    return pl.pallas_call(
        functools.partial(_attn_ctx_kernel, hb=hb, lam_init=lam_init, dk=dk),
        out_shape=jax.ShapeDtypeStruct((rows_total, mix_w), BF16),
        grid=(batch, nh),
        in_specs=[pl.BlockSpec((seq, w), lambda b, h: (b, h)), kv_spec, kv_spec,
                  pl.BlockSpec(lam_p.shape, lambda b, h: (0, 0)),
                  pl.BlockSpec((1, LANES), lambda b, h: (0, 0))],
        out_specs=pl.BlockSpec((seq, w), lambda b, h: (b, h)),
        scratch_shapes=[pltpu.VMEM((2 * seq, seq), F32)],
        compiler_params=_cparams(("arbitrary", "arbitrary")),
        name="attn_ctx",
    )(q, k_new, v_new, lam_p, gain)


def _attention_lat(qk, v, cache_k, cache_v, l, lam_p, gain, mix, *, row0, batch, seq, heads, lam_init, dk):
    w = LANES
    tq = _pick(seq, 512)
    nq = seq // tq
    i0 = row0 // tq
    past = cache_k.shape[2]
    kchunk = _pick(seq, 1024)
    n_sub = 2 if tq % (2 * 2 * SUBLANES) == 0 else 1
    cache_spec = pl.BlockSpec((None, None, past, w), lambda b, h, i: (b, l, 0, h))
    return pl.pallas_call(
        functools.partial(_attn_lat_kernel, lam_init=lam_init, dk=dk, kchunk=kchunk),
        out_shape=jax.ShapeDtypeStruct(mix.shape, mix.dtype),
        grid=(batch, heads, nq),
        in_specs=[pl.BlockSpec((tq, w), lambda b, h, i: (b * nq + i, h)),
                  pl.BlockSpec((seq, w), lambda b, h, i: (b, heads + h)),
                  pl.BlockSpec((seq, w), lambda b, h, i: (b, h)),
                  cache_spec, cache_spec,
                  pl.BlockSpec(lam_p.shape, lambda b, h, i: (0, 0)),
                  pl.BlockSpec((1, LANES), lambda b, h, i: (0, 0)),
                  pl.BlockSpec(memory_space=pl.ANY)],
        out_specs=pl.BlockSpec((tq, w), lambda b, h, i: (i0 + b * nq + i, h)),
        scratch_shapes=[pltpu.VMEM((2 * tq // n_sub, past + seq), F32)] * n_sub,
        input_output_aliases={7: 0},
        compiler_params=_cparams(("arbitrary", "arbitrary", "arbitrary")),
        name="attn_lat",
    )(qk, qk, v, cache_k, cache_v, lam_p, gain, mix)


def _conv_kernel(xc_ref, xp_ref, xn_ref, w_ref, b_ref, o_ref, buf, *, conv_w, n_ctx_chunks, cpc, cpl):
    i = pl.program_id(0)
    in_ctx = i < n_ctx_chunks
    pos = jnp.where(in_ctx, i % cpc, (i - n_ctx_chunks) % cpl)
    last = jnp.where(in_ctx, cpc - 1, cpl - 1)
    prev = xp_ref[...]
    nxt = xn_ref[...]
    cb = xc_ref.shape[0]
    buf[0:HALO, :] = jnp.where(pos == 0, jnp.zeros_like(prev), prev)
    buf[HALO:HALO + cb, :] = xc_ref[...]
    buf[HALO + cb:, :] = jnp.where(pos == last, jnp.zeros_like(nxt), nxt)
    pad = (conv_w - 1) // 2
    acc = None
    for k in range(conv_w):
        term = buf[HALO - pad + k:HALO - pad + k + cb, :] * w_ref[k:k + 1, :]
        acc = term if acc is None else acc + term
    acc = acc + b_ref[...]
    o_ref[...] = acc * _sigmoid(acc)


def _conv_silu(xbc, conv_w_l, conv_b_l, *, conv_ch, n_ctx, seq_ctx, seq_lat):
    m = xbc.shape[0]
    kw = conv_w_l.shape[0]
    nb = m // HALO
    cb = math.gcd(math.gcd(seq_ctx, seq_lat), CONV_ROWS)
    per = cb // HALO
    return pl.pallas_call(
        functools.partial(_conv_kernel, conv_w=kw, n_ctx_chunks=n_ctx // cb, cpc=seq_ctx // cb, cpl=seq_lat // cb),
        out_shape=jax.ShapeDtypeStruct((m, conv_ch), F32),
        grid=(m // cb,),
        in_specs=[pl.BlockSpec((cb, conv_ch), lambda i: (i, 0)),
                  pl.BlockSpec((HALO, conv_ch), lambda i: (jnp.maximum(i * per - 1, 0), 0)),
                  pl.BlockSpec((HALO, conv_ch), lambda i: (jnp.minimum((i + 1) * per, nb - 1), 0)),
                  pl.BlockSpec((kw, conv_ch), lambda i: (0, 0)),
                  pl.BlockSpec((1, conv_ch), lambda i: (0, 0))],
        out_specs=pl.BlockSpec((cb, conv_ch), lambda i: (i, 0)),
        scratch_shapes=[pltpu.VMEM((cb + 2 * HALO, conv_ch), F32)],
        compiler_params=_cparams(("arbitrary",)),
        name="conv_silu",
    )(xbc, xbc, xbc, conv_w_l, conv_b_l.reshape(1, conv_ch))


def _split3(x):
    hi = x.astype(BF16)
    r1 = x - hi.astype(F32)
    mid = r1.astype(BF16)
    lo = (r1 - mid.astype(F32)).astype(BF16)
    return hi, mid, lo


def _tri_sum(tri, x):
    hi, mid, lo = _split3(x)
    return (jnp.dot(tri, hi, preferred_element_type=F32) + jnp.dot(tri, mid, preferred_element_type=F32)
            + jnp.dot(tri, lo, preferred_element_type=F32))


def _colb(x, j):
    return jnp.broadcast_to(x[:, j:j + 1], (x.shape[0], LANES))


def _scan_dir(xa_ref, dt_ref, bias, a_neg, h_ref, y_ref, *, backward, n_heads, hpg, pdim, ssm_w, bc_cols):
    q = CHUNK
    ri = lax.broadcasted_iota(jnp.int32, (q, q), 0)
    ci = lax.broadcasted_iota(jnp.int32, (q, q), 1)
    keep = (ci >= ri) if backward else (ri >= ci)
    tri = jnp.where(keep, 1.0, 0.0).astype(BF16)
    lane = lax.broadcasted_iota(jnp.int32, (q, LANES), 1)
    first_half = lane < pdim

    raw = dt_ref[...] + bias
    dt = jnp.maximum(raw, 0.0) + jnp.log(1.0 + jnp.exp(-jnp.abs(raw)))
    acum = _tri_sum(tri, dt * a_neg)
    edge = acum[0:1, :] if backward else acum[q - 1:q, :]
    wend = jnp.exp(edge - acum) * dt
    eac = jnp.exp(acum)
    cdec = jnp.exp(edge)
    acum_t = acum.T
    dt_t = dt.T

    lane0 = n_heads if backward else 0
    heads_per_tile = LANES // pdim
    gmat = {}
    for t in range(n_heads // heads_per_tile):
        xs = xa_ref[:, t * LANES:(t + 1) * LANES]
        xs_b = xs.astype(BF16)
        y_intra = []
        w_cols, e_cols, d_cols = [], [], []
        grp = (t * heads_per_tile) // hpg
        if grp not in gmat:
            bm = xa_ref[:, ssm_w + grp * LANES:ssm_w + (grp + 1) * LANES].astype(BF16)
            cm = xa_ref[:, ssm_w + bc_cols + grp * LANES:ssm_w + bc_cols + (grp + 1) * LANES].astype(BF16)
            gmat = {grp: (bm, cm, lax.dot_general(cm, bm, _NT, preferred_element_type=F32))}
        bm, cm, g = gmat[grp]
        for hh in range(heads_per_tile):
            j = lane0 + t * heads_per_tile + hh
            seg = _colb(acum, j) - acum_t[j:j + 1, :]
            decay = jnp.exp(jnp.where(keep, seg, -jnp.inf))
            scores = (g * decay * dt_t[j:j + 1, :]).astype(BF16)
            y_intra.append(jnp.dot(scores, xs_b, preferred_element_type=F32))
            w_cols.append(_colb(wend, j))
            e_cols.append(_colb(eac, j))
            d_cols.append(jnp.broadcast_to(cdec[:, j:j + 1], (pdim, LANES)))
        if heads_per_tile == 2:
            yi = jnp.where(first_half, y_intra[0], y_intra[1])
            wp = jnp.where(first_half, w_cols[0], w_cols[1])
            ep = jnp.where(first_half, e_cols[0], e_cols[1])
        else:
            yi, wp, ep = y_intra[0], w_cols[0], e_cols[0]
        dp = jnp.concatenate(d_cols, axis=0)
        rows = slice(t * LANES, (t + 1) * LANES)
        h_old = h_ref[rows, :]
        y_inter = lax.dot_general(cm, h_old.astype(BF16), _NT, preferred_element_type=F32) * ep
        y_ref[:, t * LANES:(t + 1) * LANES] = yi + y_inter
        s_chunk = lax.dot_general((xs * wp).astype(BF16), bm, (((0,), (0,)), ((), ())),
                                  preferred_element_type=F32)
        h_ref[rows, :] = dp * h_old + s_chunk


def _scan_kernel(*refs, zero_init, emit_state, n_alias, nc, **dims):
    xf_ref, xb_ref, dtf_ref, dtb_ref, bias_ref, alog_ref = refs[:6]
    h0_ref = None if zero_init else refs[6]
    n_in = 6 + (0 if zero_init else 1) + n_alias
    yf_ref, yb_ref = refs[n_in:n_in + 2]
    hout_ref = refs[n_in + 2] if emit_state else None
    hf, hb = refs[-2:]
    c = pl.program_id(1)

    @pl.when(c == 0)
    def _():
        if zero_init:
            hf[...] = jnp.zeros_like(hf)
            hb[...] = jnp.zeros_like(hb)
        else:
            hf[...] = h0_ref[0]
            hb[...] = h0_ref[1]

    bias = bias_ref[...]
    a_neg = -jnp.exp(alog_ref[...])
    _scan_dir(xf_ref, dtf_ref, bias, a_neg, hf, yf_ref, backward=False, **dims)
    _scan_dir(xb_ref, dtb_ref, bias, a_neg, hb, yb_ref, backward=True, **dims)

    if emit_state:
        @pl.when(c == nc - 1)
        def _():
            hout_ref[0] = hf[...]
            hout_ref[1] = hb[...]


def _ssd_scan(xact, dt, bias, alog, l, *, row0, batch, seq, depth, h0=None, y_prev=None, states_prev=None,
              emit_state, n_heads, hpg, pdim, nstate, ssm_w, bc_cols):
    nc = seq // CHUNK
    c0 = row0 // CHUNK
    rows_total, conv_ch = xact.shape
    hp = n_heads * pdim
    zero_init = h0 is None
    fwd = lambda b, c: (c0 + b * nc + c, 0)
    bwd = lambda b, c: (c0 + b * nc + (nc - 1 - c), 0)
    vec = pl.BlockSpec((1, LANES), lambda b, c: (0, 0))
    in_specs = [pl.BlockSpec((CHUNK, conv_ch), fwd), pl.BlockSpec((CHUNK, conv_ch), bwd),
                pl.BlockSpec((CHUNK, LANES), fwd), pl.BlockSpec((CHUNK, LANES), bwd), vec, vec]
    args = [xact, xact, dt, dt, bias, alog]
    if not zero_init:
        in_specs.append(pl.BlockSpec((None, None, 2, hp, nstate), lambda b, c: (b, l, 0, 0, 0)))
        args.append(h0)
    aliases = {}
    if y_prev is not None:
        for k, arr in enumerate(y_prev):
            aliases[len(args)] = k
            in_specs.append(pl.BlockSpec(memory_space=pl.ANY))
            args.append(arr)
    out_shape = [jax.ShapeDtypeStruct((rows_total, ssm_w), F32), jax.ShapeDtypeStruct((rows_total, ssm_w), F32)]
    out_specs = [pl.BlockSpec((CHUNK, ssm_w), fwd), pl.BlockSpec((CHUNK, ssm_w), bwd)]
    if emit_state:
        out_shape.append(jax.ShapeDtypeStruct((batch, depth, 2, hp, nstate), F32))
        out_specs.append(pl.BlockSpec((None, None, 2, hp, nstate), lambda b, c: (b, l, 0, 0, 0)))
        if states_prev is not None:
            aliases[len(args)] = 2
            in_specs.append(pl.BlockSpec(memory_space=pl.ANY))
            args.append(states_prev)
    dims = dict(n_heads=n_heads, hpg=hpg, pdim=pdim, ssm_w=ssm_w, bc_cols=bc_cols)
    return pl.pallas_call(
        functools.partial(_scan_kernel, zero_init=zero_init, emit_state=emit_state, n_alias=len(aliases), nc=nc,
                          **dims),
        out_shape=out_shape,
        grid=(batch, nc),
        in_specs=in_specs,
        out_specs=out_specs,
        scratch_shapes=[pltpu.VMEM((hp, nstate), F32), pltpu.VMEM((hp, nstate), F32)],
        input_output_aliases=aliases,
        compiler_params=_cparams(("arbitrary", "arbitrary")),
        name="ssd_scan",
    )(*args)


def _ssd_out_kernel(yf_ref, yb_ref, xs_ref, z_ref, d_ref, g_ref, mix_ref, o_ref):
    del mix_ref
    y = yf_ref[...] + yb_ref[...] + d_ref[...] * xs_ref[...]
    z = z_ref[...]
    t = y * (z * _sigmoid(z))
    ms = jnp.mean(t * t, axis=-1, keepdims=True)
    o_ref[...] = (t * lax.rsqrt(ms + RMS_EPS) * g_ref[...]).astype(BF16)


def _ssd_out(yf, yb, xact, z, d_lanes, gain, mix, *, ssm_w, attn_w):
    m = yf.shape[0]
    tm = _pick(m, 256)
    assert attn_w % ssm_w == 0
    oblk = attn_w // ssm_w
    tile = pl.BlockSpec((tm, ssm_w), lambda i: (i, 0))
    vec = pl.BlockSpec((1, ssm_w), lambda i: (0, 0))
    return pl.pallas_call(
        _ssd_out_kernel,
        out_shape=jax.ShapeDtypeStruct(mix.shape, mix.dtype),
        grid=(m // tm,),
        in_specs=[tile, tile, tile, tile, vec, vec, pl.BlockSpec(memory_space=pl.ANY)],
        out_specs=pl.BlockSpec((tm, ssm_w), lambda i: (i, oblk)),
        input_output_aliases={6: 0},
        compiler_params=_cparams(("arbitrary",)),
        name="ssd_out",
    )(yf, yb, xact, z, d_lanes, gain, mix)


def kernel(x_prompt, x_sample, c, cache_k, cache_v, state_ssm, c_ctx, w_mod, b_mod, ln_g, ln_b, w_ffn_up, w_ffn_down, w_in, conv_w, conv_b, dt_bias, a_log, d_skip, ssm_norm_g, lambda_q1, lambda_k1, lambda_q2, lambda_k2, subln_g, w_out):
    batch, seq, d = x_prompt.shape
    dbatch, dseq, _ = x_sample.shape
    depth = w_in.shape[0]
    heads, dk = cache_k.shape[3], cache_k.shape[5]
    dv = cache_v.shape[4]
    assert dv == LANES and 2 * dk == LANES, "one attention head must fill one 128-lane tile"
    attn_w = heads * dv
    n_heads, pdim, nstate = state_ssm.shape[3], state_ssm.shape[4], state_ssm.shape[5]
    assert nstate == LANES and LANES % pdim == 0 and 2 * n_heads <= LANES
    ssm_w = n_heads * pdim
    conv_ch = conv_w.shape[2]
    bc_cols = (conv_ch - ssm_w) // 2
    hpg = n_heads // (bc_cols // nstate)
    assert hpg % (LANES // pdim) == 0
    dff = w_ffn_down.shape[2]
    fp = -(-dff // 512) * 512
    alpha = (2 * depth) ** 0.25
    n_ctx, n_lat = batch * seq, dbatch * dseq
    assert seq % CHUNK == 0 and dseq % CHUNK == 0 and dseq % GRID_W == 0
    dt_cols = 2 * n_heads

    group_rows = math.gcd(n_ctx, dseq)

    def row_of_tile(i, tm):
        return jnp.where(i * tm < n_ctx, 0, 1 + (i * tm - n_ctx) // dseq)

    pad_f = fp - dff
    w_up = jnp.pad(w_ffn_up.reshape(depth, 2, d, 2, dff), ((0, 0),) * 4 + ((0, pad_f),)).astype(BF16)
    w_up = w_up.reshape(depth, 2, d, 2 * fp)
    w_down = jnp.pad(w_ffn_down, ((0, 0), (0, 0), (0, pad_f), (0, 0))).astype(BF16)
    w_in_b = jnp.pad(w_in, ((0, 0), (0, 0), (0, LANES - dt_cols))).astype(BF16)
    w_out_b = w_out.astype(BF16)
    col_z = 3 * attn_w
    col_x = col_z + ssm_w
    col_dt = col_x + conv_ch

    cvec = jnp.concatenate([c_ctx[None, :], c, jnp.zeros((MOD_ROWS - 1 - dbatch, d), F32)], axis=0)
    mod = _modulation(cvec, w_mod, b_mod)

    x = jnp.concatenate([x_prompt.reshape(n_ctx, d), x_sample.reshape(n_lat, d)], axis=0)
    tables = _rope_tables(dseq, dk)
    ck = cache_k.reshape(dbatch, depth, cache_k.shape[2], attn_w)
    cv = cache_v.reshape(dbatch, depth, cache_v.shape[2], attn_w)
    h0_all = state_ssm.reshape(dbatch, depth, 2, ssm_w, nstate)
    scale = dk ** -0.5

    h = _prep(x, mod[0], row_of_tile, _pick(n_ctx, 512))
    k_new = v_new = states = None
    for l in range(depth):
        lam_init = 0.8 - 0.6 * math.exp(-0.3 * l)
        lam_p = jnp.stack([lambda_q1[l], lambda_k1[l], lambda_q2[l], lambda_k2[l]], axis=0)
        pad_l = ((0, 0), (0, LANES - dt_cols))
        bias_l = jnp.pad(dt_bias[l].reshape(1, dt_cols), pad_l)
        alog_l = jnp.pad(a_log[l].reshape(1, dt_cols), pad_l)
        d_lanes = jnp.repeat(d_skip[l], pdim).reshape(1, ssm_w)

        act = _ffn_up(h, w_up, l, 0, fp)
        x, h = _mm_ln(act, w_down, (l, 0), x, mod[l], mod[l], ln_g[l, 0], ln_b[l, 0], row_of_tile,
                      sub=0, coef=0.5, alpha=alpha, tile_rows=group_rows)

        z = _proj(h, w_in_b, l, row0=0, rows=n_ctx + n_lat, col0=col_z, cols=ssm_w, mode="f32")
        xbc = _proj(h, w_in_b, l, row0=0, rows=n_ctx + n_lat, col0=col_x, cols=conv_ch, mode="f32")
        dt =_proj(h, w_in_b, l, row0=0, rows=n_ctx + n_lat, col0=col_dt, cols=LANES, mode="f32")
        q_ctx = _proj(h, w_in_b, l, row0=0, rows=n_ctx, col0=0, cols=attn_w, mode="bf16", scale=scale)
        k_new = _proj(h, w_in_b, l, row0=0, rows=n_ctx, col0=attn_w, cols=attn_w, mode="cache", seq=seq,
                      cache=k_new, depth=depth)
        v_new = _proj(h, w_in_b, l, row0=0, rows=n_ctx, col0=2 * attn_w, cols=attn_w, mode="cache", seq=seq,
                      cache=v_new, depth=depth)
        qk_lat = _proj(h, w_in_b, l, row0=n_ctx, rows=n_lat, col0=0, cols=2 * attn_w, mode="rope",
                       scale=scale, n_scaled_cols=attn_w, tables=tables, seq=dseq)
        v_lat = _proj(h, w_in_b, l, row0=n_ctx, rows=n_lat, col0=2 * attn_w, cols=attn_w, mode="bf16")

        gain = subln_g[l].reshape(1, dv)
        mix = _attention_ctx(q_ctx, k_new, v_new, l, lam_p, gain, rows_total=n_ctx + n_lat,
                             mix_w=attn_w + ssm_w, batch=batch, seq=seq, heads=heads, lam_init=lam_init, dk=dk)
        mix = _attention_lat(qk_lat, v_lat, ck, cv, l, lam_p, gain, mix, row0=n_ctx, batch=dbatch, seq=dseq,
                             heads=heads, lam_init=lam_init, dk=dk)

        xact = _conv_silu(xbc, conv_w[l],conv_b[l], conv_ch=conv_ch, n_ctx=n_ctx, seq_ctx=seq, seq_lat=dseq)
        sdims = dict(depth=depth, n_heads=n_heads, hpg=hpg, pdim=pdim, nstate=nstate, ssm_w=ssm_w,
                     bc_cols=bc_cols)
        yf, yb, states = _ssd_scan(xact, dt, bias_l, alog_l, l, row0=0, batch=batch, seq=seq,
                                   states_prev=states, emit_state=True, **sdims)
        yf, yb = _ssd_scan(xact, dt, bias_l, alog_l, l, row0=n_ctx, batch=dbatch, seq=dseq, h0=h0_all,
                           y_prev=(yf, yb), emit_state=False, **sdims)
        mix = _ssd_out(yf, yb, xact, z, d_lanes, ssm_norm_g[l].reshape(1, ssm_w), mix, ssm_w=ssm_w,
                       attn_w=attn_w)

        x, h = _mm_ln(mix, w_out_b, (l,), x, mod[l], mod[l], ln_g[l, 1], ln_b[l, 1], row_of_tile,
                      sub=1, coef=1.0, alpha=alpha, tile_rows=group_rows)

        act = _ffn_up(h, w_up, l, 1, fp)
        mod_next = mod[l + 1] if l + 1 < depth else None
        x, h = _mm_ln(act, w_down, (l, 1), x, mod[l], mod_next, ln_g[l, 2], ln_b[l, 2], row_of_tile,
                      sub=2, coef=0.5, alpha=alpha, tile_rows=group_rows)

    y_prompt = x[:n_ctx].reshape(batch, seq, d)
    y_sample = x[n_ctx:].reshape(dbatch, dseq, d)
    return (y_prompt, y_sample, k_new.reshape(batch, depth, seq, heads, 2, dk),
            v_new.reshape(batch, depth, seq, heads, dv), states.reshape(batch, depth, 2, n_heads, pdim, nstate))
```
